```python
import math
import jax
import jax.numpy as jnp
from jax import lax
import numpy as np

D_MODEL = 4096
BATCH = 4
SEQ = 2048
DEPTH = 2
DEC_BATCH = 8
DEC_SEQ = 1
PAST_LEN = 16384
PAGE_SIZE = 128

N_META = 16
HEAD_DIM = 128
N_LAYERS_A = (DEPTH + 1) // 2
N_LAYERS_B = DEPTH // 2
GDN_HEADS = D_MODEL // HEAD_DIM
GDN_DK = HEAD_DIM
GDN_DV = HEAD_DIM
GDN_KEY_WIDTH = GDN_HEADS * GDN_DK
GDN_VAL_WIDTH = GDN_HEADS * GDN_DV
GDN_CONV_CH = 2 * GDN_KEY_WIDTH + GDN_VAL_WIDTH
GDN_IN_WIDTH = GDN_CONV_CH + GDN_VAL_WIDTH + 2 * GDN_HEADS
CONV_WIDTH = 4
GDN_CHUNK = 64
FOX_HEADS = D_MODEL // HEAD_DIM
FOX_WIDTH = FOX_HEADS * HEAD_DIM
FOX_IN_WIDTH = 3 * FOX_WIDTH + FOX_HEADS
Q_BLOCK = 128
D_FF = -(-8 * D_MODEL // (3 * 256)) * 256
FORGET_BIAS_INIT = 8.0
EPS = 1e-6

kernel_name = 'hybrid_gdn_fox_decode_step'

F32 = jnp.float32


def rms_norm(x, w):
    xf = x.astype(F32)
    y = xf * lax.rsqrt(jnp.mean(xf * xf, axis=-1, keepdims=True) + EPS)
    return (y * w.astype(F32)).astype(x.dtype)


def l2_normalize(x):
    xf = x.astype(F32)
    return xf * lax.rsqrt(jnp.sum(xf * xf, axis=-1, keepdims=True) + EPS)


def swiglu_ffn(h, w_gu, w_down):
    gate, up = jnp.split(h @ w_gu, 2, axis=-1)
    return (jax.nn.silu(gate) * up) @ w_down


def causal_conv(hist, u, w):
    xp = jnp.concatenate([hist.astype(u.dtype), u], axis=1)
    T = u.shape[1]
    out = xp[:, 0:T] * w[0]
    for j in range(1, CONV_WIDTH):
        out = out + xp[:, j:j + T] * w[j]
    return jax.nn.silu(out), xp[:, -(CONV_WIDTH - 1):]


def gdn_project(h, w_in, conv_w, a_log, dt_bias, conv_hist):
    B, T, _ = h.shape
    proj = h @ w_in
    qkv, z, a, b = jnp.split(proj, [GDN_CONV_CH, GDN_CONV_CH + GDN_VAL_WIDTH,
                                    GDN_CONV_CH + GDN_VAL_WIDTH + GDN_HEADS], axis=-1)
    qkv_c, new_hist = causal_conv(conv_hist, qkv, conv_w)
    q, k, v = jnp.split(qkv_c, [GDN_KEY_WIDTH, 2 * GDN_KEY_WIDTH], axis=-1)
    q = l2_normalize(q.reshape(B, T, GDN_HEADS, GDN_DK)) * (GDN_DK ** -0.5)
    k = l2_normalize(k.reshape(B, T, GDN_HEADS, GDN_DK))
    v = v.reshape(B, T, GDN_HEADS, GDN_DV).astype(F32)
    beta = jax.nn.sigmoid(b.astype(F32))
    g = -jnp.exp(a_log.astype(F32)) * jax.nn.softplus(a.astype(F32) + dt_bias.astype(F32))
    return q, k, v, g, beta, z, new_hist


def gdn_chunk(S0, q, k, v, g, beta):
    L = q.shape[1]
    gc = jnp.cumsum(g, axis=1)
    gc_h = jnp.swapaxes(gc, 1, 2)
    diff = gc_h[..., :, None] - gc_h[..., None, :]
    tri = jnp.tril(jnp.ones((L, L), bool))
    strict = jnp.tril(jnp.ones((L, L), bool), -1)
    decay = jnp.where(tri, jnp.exp(jnp.where(tri, diff, 0.0)), 0.0)
    beta_h = jnp.swapaxes(beta, 1, 2)
    kk = jnp.einsum('bihk,bjhk->bhij', k, k)
    b_mat = jnp.where(strict, kk * decay, 0.0) * beta_h[..., :, None]
    g_exp = jnp.exp(gc)[..., None]
    rhs = beta[..., None] * (v - g_exp * jnp.einsum('blhk,bhkv->blhv', k, S0))
    a_mat = jnp.eye(L, dtype=F32) + b_mat
    u = lax.linalg.triangular_solve(a_mat, jnp.transpose(rhs, (0, 2, 1, 3)),
                                    left_side=True, lower=True, unit_diagonal=True)
    qk = jnp.einsum('bihk,bjhk->bhij', q, k) * decay
    o = g_exp * jnp.einsum('blhk,bhkv->blhv', q, S0) + jnp.einsum('bhij,bhjv->bihv', qk, u)
    tail = jnp.exp(gc_h[..., -1:] - gc_h)
    S_new = (jnp.exp(gc_h[..., -1])[..., None, None] * S0
             + jnp.einsum('bjhk,bhj,bhjv->bhkv', k, tail, u))
    return o, S_new


def gdn_step(S, inp):
    q, k, v, g, beta = inp
    S = S * jnp.exp(g)[..., None, None]
    u = beta[..., None] * (v - jnp.einsum('bhk,bhkv->bhv', k, S))
    S = S + jnp.einsum('bhk,bhv->bhkv', k, u)
    return S, jnp.einsum('bhk,bhkv->bhv', q, S)


def gdn_output(o, z, onorm, w_out, dtype):
    B, T = o.shape[:2]
    on = o * lax.rsqrt(jnp.mean(o * o, axis=-1, keepdims=True) + EPS) * onorm.astype(F32)
    on = on.reshape(B, T, GDN_VAL_WIDTH) * jax.nn.silu(z.astype(F32))
    return on.astype(dtype) @ w_out


def gdn_prompt(h, w_in, conv_w, a_log, dt_bias, onorm, w_out):
    B, T, _ = h.shape
    hist0 = jnp.zeros((B, CONV_WIDTH - 1, GDN_CONV_CH), h.dtype)
    q, k, v, g, beta, z, hist = gdn_project(h, w_in, conv_w, a_log, dt_bias, hist0)
    S0 = jnp.zeros((B, GDN_HEADS, GDN_DK, GDN_DV), F32)
    o_meta, S = gdn_chunk(S0, q[:, :N_META], k[:, :N_META], v[:, :N_META],
                          g[:, :N_META], beta[:, :N_META])
    n_chunks = (T - N_META) // GDN_CHUNK

    def to_chunks(a):
        a = a[:, N_META:]
        return jnp.moveaxis(a.reshape((B, n_chunks, GDN_CHUNK) + a.shape[2:]), 1, 0)

    def body(S_c, xs):
        o_c, S_c = gdn_chunk(S_c, *xs)
        return S_c, o_c

    S, o_chunks = lax.scan(body, S, (to_chunks(q), to_chunks(k), to_chunks(v),
                                     to_chunks(g), to_chunks(beta)))
    o_real = jnp.moveaxis(o_chunks, 0, 1).reshape(B, T - N_META, GDN_HEADS, GDN_DV)
    o = jnp.concatenate([o_meta, o_real], axis=1)
    return gdn_output(o, z, onorm, w_out, h.dtype), S.astype(h.dtype), hist


def gdn_sample(h, S_prev, conv_prev, w_in, conv_w, a_log, dt_bias, onorm, w_out):
    q, k, v, g, beta, z, hist = gdn_project(h, w_in, conv_w, a_log, dt_bias, conv_prev)
    xs = (jnp.moveaxis(q, 1, 0), jnp.moveaxis(k, 1, 0), jnp.moveaxis(v, 1, 0),
          jnp.moveaxis(g, 1, 0), jnp.moveaxis(beta, 1, 0))
    S, o = lax.scan(gdn_step, S_prev.astype(F32), xs)
    o = jnp.moveaxis(o, 0, 1)
    return gdn_output(o, z, onorm, w_out, h.dtype), S.astype(S_prev.dtype), hist.astype(conv_prev.dtype)


def fox_project(h, w_in, b_f):
    B, T, _ = h.shape
    q, k, v, fl = jnp.split(h @ w_in, [FOX_WIDTH, 2 * FOX_WIDTH, 3 * FOX_WIDTH], axis=-1)
    q = q.reshape(B, T, FOX_HEADS, HEAD_DIM)
    k = k.reshape(B, T, FOX_HEADS, HEAD_DIM)
    v = v.reshape(B, T, FOX_HEADS, HEAD_DIM)
    logf = jax.nn.log_sigmoid(fl.astype(F32) + b_f.astype(F32))
    return q, k, v, logf


def fox_prompt(h, w_in, b_f, w_out):
    B, T, _ = h.shape
    q, k, v, logf = fox_project(h, w_in, b_f)
    n_blocks = -(-T // Q_BLOCK)
    Tp = n_blocks * Q_BLOCK
    pad = Tp - T
    cum = jnp.cumsum(logf, axis=1)
    q_blocks = jnp.moveaxis(jnp.pad(q, ((0, 0), (0, pad), (0, 0), (0, 0)))
                            .reshape(B, n_blocks, Q_BLOCK, FOX_HEADS, HEAD_DIM), 1, 0)
    c_blocks = jnp.moveaxis(jnp.pad(cum, ((0, 0), (0, pad), (0, 0)))
                            .reshape(B, n_blocks, Q_BLOCK, FOX_HEADS), 1, 0)
    pos_blocks = jnp.arange(Tp).reshape(n_blocks, Q_BLOCK)
    key_pos = jnp.arange(T)
    cum_k = jnp.swapaxes(cum, 1, 2)
    scale = HEAD_DIM ** -0.5

    def block(args):
        qb, cb, pb = args
        s = jnp.einsum('bqhd,bkhd->bhqk', qb, k).astype(F32) * scale
        s = s + jnp.swapaxes(cb, 1, 2)[..., :, None] - cum_k[:, :, None, :]
        s = jnp.where(pb[:, None] >= key_pos[None, :], s, -jnp.inf)
        p = jax.nn.softmax(s, axis=-1)
        return jnp.einsum('bhqk,bkhd->bqhd', p.astype(v.dtype), v)

    o = lax.map(block, (q_blocks, c_blocks, pos_blocks))
    o = jnp.moveaxis(o, 0, 1).reshape(B, Tp, FOX_WIDTH)[:, :T]
    return o @ w_out, k, v, logf


def fox_sample(h, cache_k, cache_v, cache_logf, layer, page_table, w_in, b_f, w_out):
    Bd, Q, _ = h.shape
    q, k, v, logf = fox_project(h, w_in, b_f)
    P = page_table.shape[1] * PAGE_SIZE
    k_past = cache_k[layer][page_table].reshape(Bd, P, FOX_HEADS, HEAD_DIM)
    v_past = cache_v[layer][page_table].reshape(Bd, P, FOX_HEADS, HEAD_DIM)
    lf_past = cache_logf[layer][page_table].reshape(Bd, P, FOX_HEADS).astype(F32)
    suffix = lax.cumsum(lf_past, axis=1, reverse=True) - lf_past
    cum_new = jnp.swapaxes(jnp.cumsum(logf, axis=1), 1, 2)
    scale = HEAD_DIM ** -0.5
    s_past = (jnp.einsum('bqhd,bkhd->bhqk', q, k_past).astype(F32) * scale
              + cum_new[..., :, None] + jnp.swapaxes(suffix, 1, 2)[:, :, None, :])
    s_new = (jnp.einsum('bqhd,bkhd->bhqk', q, k).astype(F32) * scale
             + cum_new[..., :, None] - cum_new[..., None, :])
    s_new = jnp.where(jnp.tril(jnp.ones((Q, Q), bool)), s_new, -jnp.inf)
    p = jax.nn.softmax(jnp.concatenate([s_past, s_new], axis=-1), axis=-1)
    o = (jnp.einsum('bhqk,bkhd->bqhd', p[..., :P].astype(v_past.dtype), v_past)
         + jnp.einsum('bhqk,bkhd->bqhd', p[..., P:].astype(v.dtype), v))
    return o.reshape(Bd, Q, FOX_WIDTH) @ w_out, k, v, logf


def setup_inputs(seed: int = 0) -> dict:
    key = jax.random.key(seed)
    ks = jax.random.split(key, 24)
    n_pages = PAST_LEN // PAGE_SIZE
    n_used = DEC_BATCH * n_pages
    n_pool = n_used + max(1, n_used // 4)

    def nrm(k, shape, scale=1.0):
        return jax.random.normal(k, shape, F32) * scale

    dt = jnp.exp(jax.random.uniform(ks[15], (N_LAYERS_A, GDN_HEADS), F32,
                                    math.log(1e-3), math.log(1e-1)))
    return {
        'x_prompt': nrm(ks[0], (BATCH, SEQ, D_MODEL)),
        'x_sample': nrm(ks[1], (DEC_BATCH, DEC_SEQ, D_MODEL)),
        'cache_k': nrm(ks[2], (N_LAYERS_B, n_pool, PAGE_SIZE, FOX_HEADS, HEAD_DIM)),
        'cache_v': nrm(ks[3], (N_LAYERS_B, n_pool, PAGE_SIZE, FOX_HEADS, HEAD_DIM)),
        'cache_logf': jax.nn.log_sigmoid(FORGET_BIAS_INIT + nrm(ks[4], (N_LAYERS_B, n_pool, PAGE_SIZE, FOX_HEADS))),
        'page_table': jax.random.permutation(ks[5], n_pool)[:n_used].reshape(DEC_BATCH, n_pages).astype(jnp.int32),
        'state_rec': nrm(ks[6], (N_LAYERS_A, DEC_BATCH, GDN_HEADS, GDN_DK, GDN_DV), GDN_DK ** -0.5),
        'state_conv': nrm(ks[7], (N_LAYERS_A, DEC_BATCH, CONV_WIDTH - 1, GDN_CONV_CH)),
        'meta_tokens': nrm(ks[8], (N_META, D_MODEL)),
        'norm_mix': 1.0 + 0.02 * nrm(ks[9], (DEPTH, D_MODEL)),
        'norm_ffn': 1.0 + 0.02 * nrm(ks[10], (DEPTH, D_MODEL)),
        'norm_final': 1.0 + 0.02 * nrm(ks[11], (D_MODEL,)),
        'w_in_a': nrm(ks[12], (N_LAYERS_A, D_MODEL, GDN_IN_WIDTH), D_MODEL ** -0.5),
        'conv_w_a': nrm(ks[13], (N_LAYERS_A, CONV_WIDTH, GDN_CONV_CH), CONV_WIDTH ** -0.5),
        'a_log_a': jnp.log(jax.random.uniform(ks[14], (N_LAYERS_A, GDN_HEADS), F32, 1.0, 16.0)),
        'dt_bias_a': dt + jnp.log(-jnp.expm1(-dt)),
        'onorm_a': 1.0 + 0.02 * nrm(ks[16], (N_LAYERS_A, GDN_DV)),
        'w_out_a': nrm(ks[17], (N_LAYERS_A, GDN_VAL_WIDTH, D_MODEL), GDN_VAL_WIDTH ** -0.5),
        'w_in_b': nrm(ks[18], (N_LAYERS_B, D_MODEL, FOX_IN_WIDTH), D_MODEL ** -0.5),
        'b_f': FORGET_BIAS_INIT + 0.5 * nrm(ks[19], (N_LAYERS_B, FOX_HEADS)),
        'w_out_b': nrm(ks[20], (N_LAYERS_B, FOX_WIDTH, D_MODEL), FOX_WIDTH ** -0.5),
        'w_gu': nrm(ks[21], (DEPTH, D_MODEL, 2 * D_FF), D_MODEL ** -0.5),
        'w_down': nrm(ks[22], (DEPTH, D_FF, D_MODEL), D_FF ** -0.5),
    }


def reference(x_prompt, x_sample, cache_k, cache_v, cache_logf, page_table, state_rec, state_conv,
              meta_tokens, norm_mix, norm_ffn, norm_final, w_in_a, conv_w_a, a_log_a, dt_bias_a,
              onorm_a, w_out_a, w_in_b, b_f, w_out_b, w_gu, w_down):
    B = x_prompt.shape[0]
    meta = jnp.broadcast_to(meta_tokens.astype(x_prompt.dtype)[None], (B, N_META, D_MODEL))
    hp = jnp.concatenate([meta, x_prompt], axis=1)
    hs = x_sample
    kp_l, vp_l, lp_l, ks_l, vs_l, ls_l = [], [], [], [], [], []
    rp_l, cp_l, rs_l, cs_l = [], [], [], []
    for i in range(DEPTH):
        j = i // 2
        ap = rms_norm(hp, norm_mix[i])
        a_s = rms_norm(hs, norm_mix[i])
        if i % 2 == 0:
            mp, rp, cp = gdn_prompt(ap, w_in_a[j], conv_w_a[j], a_log_a[j], dt_bias_a[j],
                                    onorm_a[j], w_out_a[j])
            ms, rs, cs = gdn_sample(a_s, state_rec[j], state_conv[j], w_in_a[j], conv_w_a[j],
                                    a_log_a[j], dt_bias_a[j], onorm_a[j], w_out_a[j])
            rp_l.append(rp)
            cp_l.append(cp)
            rs_l.append(rs)
            cs_l.append(cs)
        else:
            mp, kp, vp, lp = fox_prompt(ap, w_in_b[j], b_f[j], w_out_b[j])
            ms, k_s, v_s, l_s = fox_sample(a_s, cache_k, cache_v, cache_logf, j, page_table,
                                           w_in_b[j], b_f[j], w_out_b[j])
            kp_l.append(kp)
            vp_l.append(vp)
            lp_l.append(lp.astype(cache_logf.dtype))
            ks_l.append(k_s)
            vs_l.append(v_s)
            ls_l.append(l_s.astype(cache_logf.dtype))
        hp = hp + mp
        hs = hs + ms
        hp = hp + swiglu_ffn(rms_norm(hp, norm_ffn[i]), w_gu[i], w_down[i])
        hs = hs + swiglu_ffn(rms_norm(hs, norm_ffn[i]), w_gu[i], w_down[i])
    y_prompt = rms_norm(hp, norm_final)[:, N_META:]
    y_sample = rms_norm(hs, norm_final)
    return (y_prompt, y_sample,
            jnp.stack(kp_l), jnp.stack(vp_l), jnp.stack(lp_l),
            jnp.stack(ks_l), jnp.stack(vs_l), jnp.stack(ls_l),
            jnp.stack(rp_l), jnp.stack(cp_l), jnp.stack(rs_l), jnp.stack(cs_l))
```

```python
import functools
import math

import jax
import jax.numpy as jnp
from jax import lax
from jax.experimental import pallas as pl
from jax.experimental.pallas import tpu as pltpu

F32 = jnp.float32
BF16 = jnp.bfloat16
HI = lax.Precision.HIGHEST

HEAD = 128
N_META_TOK = 16
CONV_TAPS = 4
GDN_CHUNK = 64
PAGE = 128
EPS = 1e-6

VMEM_BYTES_V7X = 64 << 20
VMEM_LIMIT_MAX = VMEM_BYTES_V7X - (8 << 20)
BF16_SUBLANES = 16
LANES = 128


def _dot(a, b, precision=None):
    return jnp.dot(a, b, preferred_element_type=F32, precision=precision)


def _dot_nt(a, b, precision=None):
    return lax.dot_general(a, b, (((1,), (1,)), ((), ())), preferred_element_type=F32,
                           precision=precision)


def _dot_tn(a, b, precision=None):
    return lax.dot_general(a, b, (((0,), (0,)), ((), ())), preferred_element_type=F32,
                           precision=precision)


def _sigmoid(x):
    return 1.0 / (1.0 + jnp.exp(-x))


def _silu(x):
    return x * _sigmoid(x)


def _softplus(x):
    return jnp.maximum(x, 0.0) + jnp.log1p(jnp.exp(-jnp.abs(x)))


def _log_sigmoid(x):
    return -_softplus(-x)


def _divisor(n, cap, align):
    best = None
    for d in range(align, min(n, cap) + 1, align):
        if n % d == 0:
            best = d
    return n if best is None else best


def _params(semantics, vmem_bytes):
    limit = int(min(max(vmem_bytes * 5 // 4 + (4 << 20), 32 << 20), VMEM_LIMIT_MAX))
    return pltpu.CompilerParams(dimension_semantics=semantics, vmem_limit_bytes=limit)


def _rmsnorm_kernel(x_ref, w_ref, o_ref):
    x = x_ref[...]
    y = x * lax.rsqrt(jnp.mean(x * x, axis=-1, keepdims=True) + EPS)
    o_ref[...] = (y * w_ref[...]).astype(o_ref.dtype)


def rmsnorm(x, w, out_dtype):
    M, D = x.shape
    bm = _divisor(M, max(BF16_SUBLANES, (3 << 20) // (4 * D)), BF16_SUBLANES)
    vmem = 2 * bm * D * (4 + jnp.dtype(out_dtype).itemsize)
    return pl.pallas_call(
        _rmsnorm_kernel,
        out_shape=jax.ShapeDtypeStruct((M, D), out_dtype),
        grid=(M // bm,),
        in_specs=[pl.BlockSpec((bm, D), lambda i: (i, 0)),
                  pl.BlockSpec((1, D), lambda i: (0, 0))],
        out_specs=pl.BlockSpec((bm, D), lambda i: (i, 0)),
        compiler_params=_params(("parallel",), vmem),
        name="rmsnorm",
    )(x, w.reshape(1, D).astype(F32))


def _mm_kernel(x_ref, w_ref, o_ref):
    o_ref[...] = _dot(x_ref[...], w_ref[...]).astype(o_ref.dtype)


def _mm_res_kernel(x_ref, w_ref, r_ref, o_ref):
    o_ref[...] = (r_ref[...] + _dot(x_ref[...], w_ref[...])).astype(o_ref.dtype)


def _mm_swiglu_kernel(x_ref, wg_ref, wu_ref, o_ref):
    x = x_ref[...]
    g = _dot(x, wg_ref[...])
    u = _dot(x, wu_ref[...])
    o_ref[...] = (_silu(g) * u).astype(o_ref.dtype)


def _mm_tiles(M, K, N, n_weights=1):
    bm = _divisor(M, max(BF16_SUBLANES, (16 << 20) // (2 * K)), BF16_SUBLANES)
    if N % LANES:
        bn = N
    else:
        bn = _divisor(N, min(512, max(LANES, (6 << 20) // (2 * K * n_weights))), LANES)
    return bm, bn


def matmul(x, w, residual=None, out_dtype=F32):
    M, K = x.shape
    N = w.shape[1]
    bm, bn = _mm_tiles(M, K, N)
    in_specs = [pl.BlockSpec((bm, K), lambda i, j: (i, 0)),
                pl.BlockSpec((K, bn), lambda i, j: (0, j))]
    args = [x, w]
    vmem = 2 * (bm * K * 2 + K * bn * 2 + bm * bn * jnp.dtype(out_dtype).itemsize) + 3 * bm * bn * 4
    body = _mm_kernel
    if residual is not None:
        in_specs.append(pl.BlockSpec((bm, bn), lambda i, j: (i, j)))
        args.append(residual)
        vmem += 2 * bm * bn * 4
        body = _mm_res_kernel
    return pl.pallas_call(
        body,
        out_shape=jax.ShapeDtypeStruct((M, N), out_dtype),
        grid=(M // bm, N // bn),
        in_specs=in_specs,
        out_specs=pl.BlockSpec((bm, bn), lambda i, j: (i, j)),
        compiler_params=_params(("parallel", "parallel"), vmem),
        name="matmul",
    )(*args)


def matmul_swiglu(x, w_gu):
    M, K = x.shape
    F = w_gu.shape[1] // 2
    bm, bn = _mm_tiles(M, K, F, n_weights=2)
    nb = F // bn
    vmem = 2 * (bm * K * 2 + 2 * K * bn * 2 + bm * bn * 2) + 3 * bm * bn * 4
    return pl.pallas_call(
        _mm_swiglu_kernel,
        out_shape=jax.ShapeDtypeStruct((M, F), BF16),
        grid=(M // bm, nb),
        in_specs=[pl.BlockSpec((bm, K), lambda i, j: (i, 0)),
                  pl.BlockSpec((K, bn), lambda i, j: (0, j)),
                  pl.BlockSpec((K, bn), lambda i, j: (0, j + nb))],
        out_specs=pl.BlockSpec((bm, bn), lambda i, j: (i, j)),
        compiler_params=_params(("parallel", "parallel"), vmem),
        name="matmul_swiglu",
    )(x, w_gu, w_gu)


def _conv_kernel(x_ref, w_ref, o_ref, *, T, heads_per_block, blocks_per_part):
    kind = pl.program_id(1) // blocks_per_part
    x = x_ref[...]
    w = w_ref[...]
    row = lax.broadcasted_iota(jnp.int32, x.shape, 0)
    acc = None
    for j in range(CONV_TAPS):
        s = CONV_TAPS - 1 - j
        xs = x if s == 0 else jnp.where(row >= s, pltpu.roll(x, s, 0), 0.0)
        term = xs * w[j:j + 1, :]
        acc = term if acc is None else acc + term
    y = _silu(acc)

    @pl.when(kind == 2)
    def _():
        o_ref[...] = y

    @pl.when(kind < 2)
    def _():
        scale = jnp.where(kind == 0, HEAD ** -0.5, 1.0).astype(F32)
        for h in range(heads_per_block):
            seg = y[:, h * HEAD:(h + 1) * HEAD]
            r = lax.rsqrt(jnp.sum(seg * seg, axis=-1, keepdims=True) + EPS)
            o_ref[:, h * HEAD:(h + 1) * HEAD] = seg * r * scale


def gdn_conv_prompt(proj, conv_w, D):
    B, T, _ = proj.shape
    cb = _divisor(D, 512, HEAD)
    vmem = 4 * T * cb * 4 + 6 * T * cb * 4
    return pl.pallas_call(
        functools.partial(_conv_kernel, T=T, heads_per_block=cb // HEAD, blocks_per_part=D // cb),
        out_shape=jax.ShapeDtypeStruct((B, T, 3 * D), F32),
        grid=(B, 3 * D // cb),
        in_specs=[pl.BlockSpec((None, T, cb), lambda b, j: (b, 0, j)),
                  pl.BlockSpec((CONV_TAPS, cb), lambda b, j: (0, j))],
        out_specs=pl.BlockSpec((None, T, cb), lambda b, j: (b, 0, j)),
        compiler_params=_params(("parallel", "parallel"), vmem),
        name="gdn_conv",
    )(proj, conv_w)


def _neumann_inverse(bm, L):
    eye = (lax.broadcasted_iota(jnp.int32, (L, L), 0) ==
           lax.broadcasted_iota(jnp.int32, (L, L), 1)).astype(F32)
    n = -bm
    t = eye + n
    p = n
    for _ in range(int(math.log2(L)) - 1):
        p = _dot(p, p, HI)
        t = t + _dot(t, p, HI)
    return t


def _gdn_kernel(alog_ref, dtb_ref, q_ref, k_ref, v_ref, z_ref, ab_ref, onorm_ref,
                o_ref, s_out_ref,
                g_scr, b_scr, w_scr, u_scr, eg_scr, kt_scr, qk_scr, qkm_scr, s_scr,
                *, T, H, n_meta, chunk):
    h = pl.program_id(1)
    n_chunks = (T - n_meta) // chunk

    ab = ab_ref[...]
    sel = lax.broadcasted_iota(jnp.int32, (2 * H, HEAD), 0)
    a_col = _dot(ab, (sel == h).astype(F32), HI)
    b_col = _dot(ab, (sel == h + H).astype(F32), HI)
    decay_rate = jnp.exp(jnp.full((1, HEAD), alog_ref[h], F32))
    g_scr[...] = -decay_rate * _softplus(a_col + dtb_ref[h])
    b_scr[...] = _sigmoid(b_col)

    def prep(off, L, qk_ref, qk_off):
        rows = pl.ds(off, L)
        k = k_ref[rows, :]
        v = v_ref[rows, :]
        q = q_ref[rows, :]
        beta = b_scr[rows, :]
        ri = lax.broadcasted_iota(jnp.int32, (L, L), 0)
        ci = lax.broadcasted_iota(jnp.int32, (L, L), 1)
        tri = ri >= ci
        strict = ri > ci
        eye = ri == ci
        gc = _dot(tri.astype(F32), g_scr[rows, :], HI)
        gc_i = gc[:, :L]
        gc_j = _dot(jnp.ones((L, L), F32), jnp.where(eye, gc_i, 0.0), HI)
        decay = jnp.where(tri, jnp.exp(jnp.where(tri, gc_i - gc_j, 0.0)), 0.0)
        kb = k.astype(BF16)
        kk = _dot_nt(kb, kb)
        bm = jnp.where(strict, kk * decay, 0.0) * beta[:, :L]
        t = _neumann_inverse(bm, L)
        eg = jnp.exp(gc)
        w_scr[rows, :] = _dot(t, beta * eg * k, HI)
        u_scr[rows, :] = _dot(t, beta * v, HI)
        eg_scr[rows, :] = eg
        kt_scr[rows, :] = k * jnp.exp(gc[L - 1:L, :] - gc)
        qk_ref[pl.ds(qk_off, L), :] = _dot_nt(q.astype(BF16), kb) * decay

    def apply(off, L, qk_ref, qk_off):
        rows = pl.ds(off, L)
        s = s_scr[...]
        sb = s.astype(BF16)
        u = u_scr[rows, :] - _dot(w_scr[rows, :].astype(BF16), sb)
        ub = u.astype(BF16)
        o = (eg_scr[rows, :] * _dot(q_ref[rows, :].astype(BF16), sb)
             + _dot(qk_ref[pl.ds(qk_off, L), :].astype(BF16), ub))
        s_scr[...] = eg_scr[pl.ds(off + L - 1, 1), :] * s + _dot_tn(kt_scr[rows, :].astype(BF16), ub)
        on = o * lax.rsqrt(jnp.mean(o * o, axis=-1, keepdims=True) + EPS) * onorm_ref[...]
        o_ref[rows, :] = (on * _silu(z_ref[rows, :])).astype(o_ref.dtype)

    def chunk_off(c):
        return pl.multiple_of(n_meta + c * chunk, BF16_SUBLANES)

    def chunk_qk_off(c):
        return pl.multiple_of(c * chunk, chunk)

    prep(0, n_meta, qkm_scr, 0)

    def prep_body(c, carry):
        prep(chunk_off(c), chunk, qk_scr, chunk_qk_off(c))
        return carry

    lax.fori_loop(0, n_chunks, prep_body, 0)

    s_scr[...] = jnp.zeros_like(s_scr)
    apply(0, n_meta, qkm_scr, 0)

    def apply_body(c, carry):
        apply(chunk_off(c), chunk, qk_scr, chunk_qk_off(c))
        return carry

    lax.fori_loop(0, n_chunks, apply_body, 0)
    s_out_ref[...] = s_scr[...]


def gdn_prompt(qkv_c, proj, ab, a_log, dt_bias, onorm, D):
    B, T, _ = qkv_c.shape
    H = D // HEAD
    assert (T - N_META_TOK) % GDN_CHUNK == 0
    n_chunks = (T - N_META_TOK) // GDN_CHUNK
    col = lambda part: pl.BlockSpec((None, T, HEAD), lambda b, h: (b, 0, part * H + h))
    smem = pl.BlockSpec(memory_space=pltpu.SMEM)
    vmem = (8 + 6) * T * HEAD * 4 + 2 * T * LANES * 4 + 2 * T * HEAD * 2 + n_chunks * GDN_CHUNK * LANES * 4
    return pl.pallas_call(
        functools.partial(_gdn_kernel, T=T, H=H, n_meta=N_META_TOK, chunk=GDN_CHUNK),
        out_shape=[jax.ShapeDtypeStruct((B, T, D), BF16),
                   jax.ShapeDtypeStruct((B, H, HEAD, HEAD), F32)],
        grid=(B, H),
        in_specs=[smem, smem, col(0), col(1), col(2), col(3),
                  pl.BlockSpec((None, T, 2 * H), lambda b, h: (b, 0, 0)),
                  pl.BlockSpec((1, HEAD), lambda b, h: (0, 0))],
        out_specs=[pl.BlockSpec((None, T, HEAD), lambda b, h: (b, 0, h)),
                   pl.BlockSpec((None, None, HEAD, HEAD), lambda b, h: (b, h, 0, 0))],
        scratch_shapes=[pltpu.VMEM((T, HEAD), F32)] * 6 + [
            pltpu.VMEM((n_chunks * GDN_CHUNK, GDN_CHUNK), F32),
            pltpu.VMEM((N_META_TOK, N_META_TOK), F32),
            pltpu.VMEM((HEAD, HEAD), F32)],
        compiler_params=_params(("parallel", "parallel"), vmem),
        name="gdn_chunk",
    )(a_log.astype(F32), dt_bias.astype(F32), qkv_c, qkv_c, qkv_c, proj, ab, onorm.reshape(1, HEAD).astype(F32))


def _gdn_step_kernel(u_ref, cs_ref, w_ref, ab_ref, alog_ref, dtb_ref, onorm_ref, s_ref,
                     o_ref, s_out_ref, q_scr, k_scr, v_scr, g_scr, b_scr, o_scr, *, H):
    u_all = u_ref[...]
    u = u_all[:3 * H]
    xc = (cs_ref[0] * w_ref[0] + cs_ref[1] * w_ref[1] + cs_ref[2] * w_ref[2]) + u * w_ref[3]
    xc = _silu(xc)
    q = xc[:H]
    k = xc[H:2 * H]
    q_scr[...] = q * lax.rsqrt(jnp.sum(q * q, axis=-1, keepdims=True) + EPS) * (HEAD ** -0.5)
    k_scr[...] = k * lax.rsqrt(jnp.sum(k * k, axis=-1, keepdims=True) + EPS)
    v_scr[...] = xc[2 * H:]
    g_scr[...] = jnp.broadcast_to(-jnp.exp(alog_ref[...]) * _softplus(ab_ref[0] + dtb_ref[...]), (H, HEAD))
    b_scr[...] = jnp.broadcast_to(_sigmoid(ab_ref[1]), (H, HEAD))
    eye = (lax.broadcasted_iota(jnp.int32, (HEAD, HEAD), 0) ==
           lax.broadcasted_iota(jnp.int32, (HEAD, HEAD), 1)).astype(F32)

    def head(h, carry):
        row = pl.ds(h, 1)
        k_row = k_scr[row, :]
        s = s_ref[h] * jnp.exp(g_scr[row, :])
        upd = b_scr[row, :] * (v_scr[row, :] - _dot(k_row, s, HI))
        k_col = _dot_nt(eye, jnp.broadcast_to(k_row, (HEAD, HEAD)), HI)
        s = s + k_col * upd
        s_out_ref[h] = s
        o_scr[row, :] = _dot(q_scr[row, :], s, HI)
        return carry

    lax.fori_loop(0, H, head, 0)
    o = o_scr[...]
    on = o * lax.rsqrt(jnp.mean(o * o, axis=-1, keepdims=True) + EPS) * onorm_ref[...]
    o_ref[...] = (on * _silu(u_all[3 * H:])).astype(o_ref.dtype)


def gdn_sample(proj, ab, state_rec, state_conv, conv_w, a_log, dt_bias, onorm, D):
    Bd = proj.shape[0]
    H = D // HEAD
    col = lambda x: x.astype(F32).reshape(H, 1)
    vmem = 2 * (4 * H * HEAD * 4 + 3 * 3 * H * HEAD * 4 + 2 * H * HEAD * HEAD * 4) + 4 * 3 * H * HEAD * 4
    return pl.pallas_call(
        functools.partial(_gdn_step_kernel, H=H),
        out_shape=[jax.ShapeDtypeStruct((Bd, H, HEAD), BF16),
                   jax.ShapeDtypeStruct((Bd, H, HEAD, HEAD), F32)],
        grid=(Bd,),
        in_specs=[pl.BlockSpec((None, 4 * H, HEAD), lambda b: (b, 0, 0)),
                  pl.BlockSpec((None, CONV_TAPS - 1, 3 * H, HEAD), lambda b: (b, 0, 0, 0)),
                  pl.BlockSpec((CONV_TAPS, 3 * H, HEAD), lambda b: (0, 0, 0)),
                  pl.BlockSpec((None, 2, H, 1), lambda b: (b, 0, 0, 0)),
                  pl.BlockSpec((H, 1), lambda b: (0, 0)),
                  pl.BlockSpec((H, 1), lambda b: (0, 0)),
                  pl.BlockSpec((1, HEAD), lambda b: (0, 0)),
                  pl.BlockSpec((None, H, HEAD, HEAD), lambda b: (b, 0, 0, 0))],
        out_specs=[pl.BlockSpec((None, H, HEAD), lambda b: (b, 0, 0)),
                   pl.BlockSpec((None, H, HEAD, HEAD), lambda b: (b, 0, 0, 0))],
        scratch_shapes=[pltpu.VMEM((H, HEAD), F32)] * 6,
        compiler_params=_params(("parallel",), vmem),
        name="gdn_step",
    )(proj.reshape(Bd, 4 * H, HEAD), state_conv.reshape(Bd, CONV_TAPS - 1, 3 * H, HEAD),
      conv_w.reshape(CONV_TAPS, 3 * H, HEAD), ab.reshape(Bd, 2, H, 1), col(a_log), col(dt_bias),
      onorm.reshape(1, HEAD).astype(F32), state_rec)


def _logf_kernel(fl_ref, bf_ref, lf_ref, cum_ref, *, T):
    carry = jnp.zeros((1, fl_ref.shape[-1]), F32)
    for r0 in range(0, T, LANES):
        n = min(LANES, T - r0)
        lf = _log_sigmoid(fl_ref[r0:r0 + n, :] + bf_ref[...])
        lf_ref[r0:r0 + n, :] = lf
        tri = (lax.broadcasted_iota(jnp.int32, (n, n), 0) >=
               lax.broadcasted_iota(jnp.int32, (n, n), 1)).astype(F32)
        cum = _dot(tri, lf, HI) + carry
        cum_ref[r0:r0 + n, :] = cum
        carry = cum[n - 1:n, :]


def fox_logf(fl, b_f):
    B, T, H = fl.shape
    spec = pl.BlockSpec((None, T, H), lambda b: (b, 0, 0))
    return pl.pallas_call(
        functools.partial(_logf_kernel, T=T),
        out_shape=[jax.ShapeDtypeStruct((B, T, H), F32)] * 2,
        grid=(B,),
        in_specs=[spec, pl.BlockSpec((1, H), lambda b: (0, 0))],
        out_specs=[spec, spec],
        compiler_params=_params(("parallel",), 6 * T * LANES * 4),
        name="fox_logf",
    )(fl, b_f.reshape(1, H).astype(F32))


def _fox_kernel(q_ref, k_ref, v_ref, cum_ref, o_ref, kb_scr, vb_scr, cq_scr, ckm_scr, ckt_scr,
                *, T, H, bq):
    h = pl.program_id(1)
    t_main = (T // bq) * bq
    tail = T - t_main
    scale = HEAD ** -0.5
    kb_scr[...] = k_ref[...].astype(BF16)
    vb_scr[...] = v_ref[...].astype(BF16)
    cum = cum_ref[...]
    cq_scr[...] = _dot(cum, (lax.broadcasted_iota(jnp.int32, (H, HEAD), 0) == h).astype(F32), HI)
    pick = (lax.broadcasted_iota(jnp.int32, (8, H), 1) == h).astype(F32)
    ckm_scr[...] = _dot_nt(pick, cum[:t_main], HI)
    if tail:
        ckt_scr[...] = _dot_nt(pick, cum[t_main:], HI)

    def block(i, carry):
        r0 = pl.multiple_of(i * bq, bq)
        rows = pl.ds(r0, bq)
        qb = q_ref[rows, :].astype(BF16)
        s = _dot_nt(qb, kb_scr[:t_main, :]) * scale
        s = s + cq_scr[rows, :][:, 0:1] - ckm_scr[0:1, :]
        qi = r0 + lax.broadcasted_iota(jnp.int32, (bq, t_main), 0)
        ki = lax.broadcasted_iota(jnp.int32, (bq, t_main), 1)
        s = jnp.where(qi >= ki, s, -jnp.inf)
        e = jnp.exp(s - jnp.max(s, axis=-1, keepdims=True))
        l = jnp.sum(e, axis=-1, keepdims=True)
        o_ref[rows, :] = (_dot(e.astype(BF16), vb_scr[:t_main, :]) / l).astype(o_ref.dtype)
        return carry

    lax.fori_loop(0, t_main // bq, block, 0)

    if tail:
        qb = q_ref[t_main:, :].astype(BF16)
        cq = cq_scr[t_main:, :][:, 0:1]
        s1 = _dot_nt(qb, kb_scr[:t_main, :]) * scale + cq - ckm_scr[0:1, :]
        s2 = _dot_nt(qb, kb_scr[t_main:, :]) * scale + cq - ckt_scr[0:1, :]
        causal = (lax.broadcasted_iota(jnp.int32, (tail, tail), 0) >=
                  lax.broadcasted_iota(jnp.int32, (tail, tail), 1))
        s2 = jnp.where(causal, s2, -jnp.inf)
        m = jnp.maximum(jnp.max(s1, axis=-1, keepdims=True), jnp.max(s2, axis=-1, keepdims=True))
        e1 = jnp.exp(s1 - m)
        e2 = jnp.exp(s2 - m)
        l = jnp.sum(e1, axis=-1, keepdims=True) + jnp.sum(e2, axis=-1, keepdims=True)
        o = _dot(e1.astype(BF16), vb_scr[:t_main, :]) + _dot(e2.astype(BF16), vb_scr[t_main:, :])
        o_ref[t_main:, :] = (o / l).astype(o_ref.dtype)


def fox_prompt(qkv, cum, D):
    B, T, _ = qkv.shape
    H = D // HEAD
    bq = LANES
    t_main = (T // bq) * bq
    tail = T - t_main
    col = lambda part: pl.BlockSpec((None, T, HEAD), lambda b, h: (b, 0, part * H + h))
    vmem = 6 * T * HEAD * 4 + 2 * T * LANES * 4 + 2 * T * HEAD * 2 + 2 * T * HEAD * 2 + T * HEAD * 4 \
        + 8 * T * 4 + 6 * bq * t_main * 4
    return pl.pallas_call(
        functools.partial(_fox_kernel, T=T, H=H, bq=bq),
        out_shape=jax.ShapeDtypeStruct((B, T, D), BF16),
        grid=(B, H),
        in_specs=[col(0), col(1), col(2), pl.BlockSpec((None, T, H), lambda b, h: (b, 0, 0))],
        out_specs=pl.BlockSpec((None, T, HEAD), lambda b, h: (b, 0, h)),
        scratch_shapes=[pltpu.VMEM((T, HEAD), BF16), pltpu.VMEM((T, HEAD), BF16),
                        pltpu.VMEM((T, HEAD), F32), pltpu.VMEM((8, t_main), F32),
                        pltpu.VMEM((8, max(tail, 8)), F32)],
        compiler_params=_params(("parallel", "parallel"), vmem),
        name="fox_prompt",
    )(qkv, qkv, qkv, cum)


def _fox_decode_kernel(pt_ref, *refs, n_pages, G, H):
    k_refs = refs[0:G]
    lf_refs = refs[G:2 * G]
    v_refs = refs[2 * G:3 * G]
    (q_ref, kn_ref, vn_ref, fl_ref, bf_ref, seg_ref, exp_ref,
     o_ref, lfn_ref, sc_scr, acc_scr, carry_scr, l_scr, pn_scr) = refs[3 * G:]
    phase = pl.program_id(1)
    g = pl.program_id(2)
    n_steps = n_pages // G
    scale = HEAD ** -0.5
    q = q_ref[...]
    lf_new = _log_sigmoid(fl_ref[...] + bf_ref[...])

    @pl.when((phase == 0) & (g == 0))
    def _():
        carry_scr[...] = jnp.zeros_like(carry_scr)
        lfn_ref[...] = lf_new

    @pl.when(phase == 0)
    def _():
        upper = (lax.broadcasted_iota(jnp.int32, (PAGE, PAGE), 0) <
                 lax.broadcasted_iota(jnp.int32, (PAGE, PAGE), 1)).astype(F32)
        for i in range(G):
            page = n_pages - 1 - (g * G + i)
            lf = lf_refs[i][...]
            suffix = _dot(upper, lf, HI) + carry_scr[...]
            carry_scr[...] = suffix[0:1, :] + lf[0:1, :]
            prod = (k_refs[i][...] * q).astype(BF16)
            s = _dot(prod, seg_ref[...]) * scale + lf_new + suffix
            sc_scr[pl.ds(pl.multiple_of(page * PAGE, PAGE), PAGE), :] = s

    @pl.when((phase == 1) & (g == 0))
    def _():
        s_new = _dot((kn_ref[...] * q).astype(BF16), seg_ref[...]) * scale + (lf_new - lf_new)
        s_all = sc_scr[...]
        m = jnp.maximum(jnp.max(s_all, axis=0, keepdims=True), s_new)
        p = jnp.exp(s_all - m)
        p_new = jnp.exp(s_new - m)
        sc_scr[...] = p
        l_scr[...] = jnp.sum(p, axis=0, keepdims=True) + p_new
        pn_scr[...] = p_new
        acc_scr[...] = jnp.zeros_like(acc_scr)

    @pl.when(phase == 1)
    def _():
        for i in range(G):
            page = n_pages - 1 - (g * G + i)
            p = sc_scr[pl.ds(pl.multiple_of(page * PAGE, PAGE), PAGE), :]
            pe = _dot(p.astype(BF16), exp_ref[...])
            c = pe * v_refs[i][...]
            part = c[0:8, :]
            for r in range(8, PAGE, 8):
                part = part + c[r:r + 8, :]
            acc_scr[...] += part

    @pl.when((phase == 1) & (g == n_steps - 1))
    def _():
        tot = jnp.sum(acc_scr[...], axis=0, keepdims=True)
        tot = tot + _dot(pn_scr[...].astype(BF16), exp_ref[...]) * vn_ref[...]
        o_ref[...] = (tot / _dot(l_scr[...], exp_ref[...].astype(F32), HI)).astype(o_ref.dtype)


def fox_decode(q, k_new, v_new, fl_new, b_f, cache_k, cache_v, cache_logf, layer, page_table, D):
    Bd = q.shape[0]
    H = D // HEAD
    n_pages = page_table.shape[1]
    n_pool = cache_k.shape[1]
    G = 4 if n_pages % 4 == 0 else 1
    n_steps = n_pages // G
    ck = cache_k.reshape(cache_k.shape[0], n_pool, PAGE, D)
    cv = cache_v.reshape(cache_v.shape[0], n_pool, PAGE, D)
    seg = (jnp.arange(D)[:, None] // HEAD == jnp.arange(H)[None, :]).astype(BF16)
    expand = seg.T

    def page_of(b, step, i, pt):
        return pt[b, n_pages - 1 - (step * G + i)]

    def k_spec(i, last):
        def index(b, ph, g, pt):
            step = jnp.where(ph == 0, g, n_steps - 1)
            return (layer, page_of(b, step, i, pt), 0, 0)
        return pl.BlockSpec((None, None, PAGE, last), index)

    def v_spec(i):
        def index(b, ph, g, pt):
            step = jnp.where(ph == 1, g, 0)
            return (layer, page_of(b, step, i, pt), 0, 0)
        return pl.BlockSpec((None, None, PAGE, D), index)

    row = lambda n: pl.BlockSpec((None, 1, n), lambda b, ph, g, pt: (b, 0, 0))
    const = lambda shape: pl.BlockSpec(shape, lambda b, ph, g, pt: (0, 0))
    vmem = 2 * 2 * G * PAGE * D * 4 + 2 * G * PAGE * LANES * 4 + n_pages * PAGE * LANES * 4 \
        + 2 * (D * LANES * 2 + H * D * 2) + 6 * PAGE * D * 4
    grid_spec = pltpu.PrefetchScalarGridSpec(
        num_scalar_prefetch=1,
        grid=(Bd, 2, n_steps),
        in_specs=([k_spec(i, D) for i in range(G)] + [k_spec(i, H) for i in range(G)]
                  + [v_spec(i) for i in range(G)]
                  + [row(D), row(D), row(D), row(H), const((1, H)), const((D, H)), const((H, D))]),
        out_specs=[row(D), row(H)],
        scratch_shapes=[pltpu.VMEM((n_pages * PAGE, H), F32), pltpu.VMEM((8, D), F32),
                        pltpu.VMEM((1, H), F32), pltpu.VMEM((1, H), F32), pltpu.VMEM((1, H), F32)],
    )
    r3 = lambda x: x.reshape(Bd, 1, x.shape[-1])
    o, lfn = pl.pallas_call(
        functools.partial(_fox_decode_kernel, n_pages=n_pages, G=G, H=H),
        out_shape=[jax.ShapeDtypeStruct((Bd, 1, D), BF16), jax.ShapeDtypeStruct((Bd, 1, H), F32)],
        grid_spec=grid_spec,
        compiler_params=_params(("parallel", "arbitrary", "arbitrary"), vmem),
        name="fox_decode",
    )(page_table, *([ck] * G), *([cache_logf] * G), *([cv] * G),
      r3(q), r3(k_new), r3(v_new), r3(fl_new), b_f.reshape(1, H).astype(F32), seg, expand)
    return o.reshape(Bd, D), lfn.reshape(Bd, H)


def _ffn(h, res, norm_w, w_gu, w_down):
    a = rmsnorm(h, norm_w, BF16)
    return matmul(matmul_swiglu(a, w_gu), w_down, residual=res)


def kernel(x_prompt, x_sample, cache_k, cache_v, cache_logf, page_table, state_rec, state_conv,
           meta_tokens, norm_mix, norm_ffn, norm_final, w_in_a, conv_w_a, a_log_a, dt_bias_a,
           onorm_a, w_out_a, w_in_b, b_f, w_out_b, w_gu, w_down):
    B, S, D = x_prompt.shape
    Bd, Q, _ = x_sample.shape
    assert Q == 1, "the sample group is a single-token step"
    H = D // HEAD
    T = S + N_META_TOK
    M = B * T
    meta = jnp.broadcast_to(meta_tokens.astype(x_prompt.dtype)[None], (B, N_META_TOK, D))
    hp = jnp.concatenate([meta, x_prompt], axis=1).reshape(M, D)
    hs = x_sample.reshape(Bd, D)
    bf = lambda w: w.astype(BF16)

    w_in = bf(w_in_a[0])
    w_qkvz, w_ab = w_in[:, :4 * D], w_in[:, 4 * D:]
    w_out = bf(w_out_a[0])
    ap = rmsnorm(hp, norm_mix[0], BF16)
    a_s = rmsnorm(hs, norm_mix[0], BF16)
    proj_p = matmul(ap, w_qkvz).reshape(B, T, 4 * D)
    ab_p = matmul(ap, w_ab).reshape(B, T, 2 * H)
    proj_s = matmul(a_s, w_qkvz)
    ab_s = matmul(a_s, w_ab)

    qkv_c = gdn_conv_prompt(proj_p, conv_w_a[0], D)
    on_p, rec_p = gdn_prompt(qkv_c, proj_p, ab_p, a_log_a[0], dt_bias_a[0], onorm_a[0], D)
    conv_p = proj_p[:, T - (CONV_TAPS - 1):, :3 * D]
    on_s, rec_s = gdn_sample(proj_s, ab_s, state_rec[0], state_conv[0], conv_w_a[0], a_log_a[0],
                             dt_bias_a[0], onorm_a[0], D)
    conv_s = jnp.concatenate([state_conv[0][:, 1:], proj_s[:, None, :3 * D].astype(state_conv.dtype)], axis=1)

    hp = matmul(on_p.reshape(M, D), w_out, residual=hp)
    hs = matmul(on_s.reshape(Bd, D), w_out, residual=hs)
    w_gu0, w_dn0 = bf(w_gu[0]), bf(w_down[0])
    hp = _ffn(hp, hp, norm_ffn[0], w_gu0, w_dn0)
    hs = _ffn(hs, hs, norm_ffn[0], w_gu0, w_dn0)

    w_in = bf(w_in_b[0])
    w_qkv, w_f = w_in[:, :3 * D], w_in[:, 3 * D:]
    w_out = bf(w_out_b[0])
    ap = rmsnorm(hp, norm_mix[1], BF16)
    a_s = rmsnorm(hs, norm_mix[1], BF16)
    qkv_p = matmul(ap, w_qkv).reshape(B, T, 3 * D)
    fl_p = matmul(ap, w_f).reshape(B, T, H)
    qkv_s = matmul(a_s, w_qkv)
    fl_s = matmul(a_s, w_f)

    lf_p, cum_p = fox_logf(fl_p, b_f[0])
    o_p = fox_prompt(qkv_p, cum_p, D)
    o_s, lf_s = fox_decode(qkv_s[:, :D], qkv_s[:, D:2 * D], qkv_s[:, 2 * D:], fl_s, b_f[0],
                           cache_k, cache_v, cache_logf, 0, page_table, D)

    hp = matmul(o_p.reshape(M, D), w_out, residual=hp)
    hs = matmul(o_s, w_out, residual=hs)
    w_gu1, w_dn1 = bf(w_gu[1]), bf(w_down[1])
    hp = _ffn(hp, hp, norm_ffn[1], w_gu1, w_dn1)
    hs = _ffn(hs, hs, norm_ffn[1], w_gu1, w_dn1)

    y_prompt = rmsnorm(hp, norm_final, F32).reshape(B, T, D)[:, N_META_TOK:]
    y_sample = rmsnorm(hs, norm_final, F32).reshape(Bd, 1, D)

    k_prompt = qkv_p[:, :, D:2 * D].reshape(1, B, T, H, HEAD)
    v_prompt = qkv_p[:, :, 2 * D:].reshape(1, B, T, H, HEAD)
    logf_prompt = lf_p.astype(cache_logf.dtype)[None]
    k_sample = qkv_s[:, D:2 * D].reshape(1, Bd, 1, H, HEAD)
    v_sample = qkv_s[:, 2 * D:].reshape(1, Bd, 1, H, HEAD)
    logf_sample = lf_s.astype(cache_logf.dtype).reshape(1, Bd, 1, H)
    return (y_prompt, y_sample, k_prompt, v_prompt, logf_prompt, k_sample, v_sample, logf_sample,
            rec_p[None], conv_p[None], rec_s.astype(state_rec.dtype)[None], conv_s[None])
```

```python
import functools
import math

import jax
import jax.numpy as jnp
from jax import lax
from jax.experimental import pallas as pl
from jax.experimental.pallas import tpu as pltpu

F32 = jnp.float32
BF16 = jnp.bfloat16
HI = lax.Precision.HIGHEST

HEAD = 128
N_META_TOK = 16
CONV_TAPS = 4
GDN_CHUNK = 64
PAGE = 128
EPS = 1e-6

VMEM_BYTES_V7X = 64 << 20
VMEM_LIMIT_MAX = VMEM_BYTES_V7X - (8 << 20)
BF16_SUBLANES = 16
LANES = 128


def _dot(a, b, precision=None):
    return jnp.dot(a, b, preferred_element_type=F32, precision=precision)


def _dot_nt(a, b, precision=None):
    return lax.dot_general(a, b, (((1,), (1,)), ((), ())), preferred_element_type=F32,
                           precision=precision)


def _dot_tn(a, b, precision=None):
    return lax.dot_general(a, b, (((0,), (0,)), ((), ())), preferred_element_type=F32,
                           precision=precision)


def _split2(x):
    hi = x.astype(BF16)
    return hi, (x - hi.astype(F32)).astype(BF16)


def _split3(x):
    hi = x.astype(BF16)
    r = x - hi.astype(F32)
    mid = r.astype(BF16)
    return hi, mid, (r - mid.astype(F32)).astype(BF16)


def _dot_sel(sel, x, dot=_dot):
    hi, mid, lo = _split3(x)
    return (dot(sel, hi) + dot(sel, mid)) + dot(sel, lo)


def _dot_sel_r(x, sel, dot=_dot):
    hi, mid, lo = _split3(x)
    return (dot(hi, sel) + dot(mid, sel)) + dot(lo, sel)


def _dot_x3(a, b):
    ah, al = _split2(a)
    bh, bl = _split2(b)
    return _dot(ah, bh) + (_dot(ah, bl) + _dot(al, bh))


def _sigmoid(x):
    return 1.0 / (1.0 + jnp.exp(-x))


def _silu(x):
    return x * _sigmoid(x)


def _softplus(x):
    return jnp.maximum(x, 0.0) + jnp.log1p(jnp.exp(-jnp.abs(x)))


def _log_sigmoid(x):
    return -_softplus(-x)


def _divisor(n, cap, align):
    best = None
    for d in range(align, min(n, cap) + 1, align):
        if n % d == 0:
            best = d
    return n if best is None else best


def _params(semantics, vmem_bytes):
    limit = int(min(max(vmem_bytes * 5 // 4 + (4 << 20), 32 << 20), VMEM_LIMIT_MAX))
    return pltpu.CompilerParams(dimension_semantics=semantics, vmem_limit_bytes=limit)


def _rmsnorm_kernel(x_ref, w_ref, o_ref):
    x = x_ref[...]
    y = x * lax.rsqrt(jnp.mean(x * x, axis=-1, keepdims=True) + EPS)
    o_ref[...] = (y * w_ref[...]).astype(o_ref.dtype)


def rmsnorm(x, w, out_dtype):
    M, D = x.shape
    bm = _divisor(M, max(BF16_SUBLANES, (3 << 20) // (4 * D)), BF16_SUBLANES)
    vmem = 2 * bm * D * (4 + jnp.dtype(out_dtype).itemsize)
    return pl.pallas_call(
        _rmsnorm_kernel,
        out_shape=jax.ShapeDtypeStruct((M, D), out_dtype),
        grid=(M // bm,),
        in_specs=[pl.BlockSpec((bm, D), lambda i: (i, 0)),
                  pl.BlockSpec((1, D), lambda i: (0, 0))],
        out_specs=pl.BlockSpec((bm, D), lambda i: (i, 0)),
        compiler_params=_params(("parallel",), vmem),
        name="rmsnorm",
    )(x, w.reshape(1, D).astype(F32))


def _mm_kernel(x_ref, w_ref, o_ref):
    o_ref[...] = _dot(x_ref[...], w_ref[...]).astype(o_ref.dtype)


def _mm_res_kernel(x_ref, w_ref, r_ref, o_ref):
    o_ref[...] = (r_ref[...] + _dot(x_ref[...], w_ref[...])).astype(o_ref.dtype)


def _mm_swiglu_kernel(x_ref, wg_ref, wu_ref, o_ref):
    x = x_ref[...]
    g = _dot(x, wg_ref[...])
    u = _dot(x, wu_ref[...])
    o_ref[...] = (_silu(g) * u).astype(o_ref.dtype)


def _mm_tiles(M, K, N, n_weights=1):
    bm = _divisor(M, max(BF16_SUBLANES, (16 << 20) // (2 * K)), BF16_SUBLANES)
    if N % LANES:
        bn = N
    else:
        bn = _divisor(N, min(512, max(LANES, (6 << 20) // (2 * K * n_weights))), LANES)
    return bm, bn


def matmul(x, w, residual=None, out_dtype=F32, cols=None):
    M, K = x.shape
    col0, N = (0, w.shape[1]) if cols is None else cols
    bm, bn = _mm_tiles(M, K, N)
    assert col0 % bn == 0 and col0 + N <= w.shape[1]
    j0 = col0 // bn
    in_specs = [pl.BlockSpec((bm, K), lambda i, j: (i, 0)),
                pl.BlockSpec((K, bn), lambda i, j: (0, j + j0))]
    args = [x, w]
    vmem = 2 * (bm * K * 2 + K * bn * 2 + bm * bn * jnp.dtype(out_dtype).itemsize) + 3 * bm * bn * 4
    body = _mm_kernel
    if residual is not None:
        in_specs.append(pl.BlockSpec((bm, bn), lambda i, j: (i, j)))
        args.append(residual)
        vmem += 2 * bm * bn * 4
        body = _mm_res_kernel
    return pl.pallas_call(
        body,
        out_shape=jax.ShapeDtypeStruct((M, N), out_dtype),
        grid=(M // bm, N // bn),
        in_specs=in_specs,
        out_specs=pl.BlockSpec((bm, bn), lambda i, j: (i, j)),
        compiler_params=_params(("parallel", "parallel"), vmem),
        name="matmul",
    )(*args)


def matmul_swiglu(x, w_gu):
    M, K = x.shape
    F = w_gu.shape[1] // 2
    bm, bn = _mm_tiles(M, K, F, n_weights=2)
    nb = F // bn
    vmem = 2 * (bm * K * 2 + 2 * K * bn * 2 + bm * bn * 2) + 3 * bm * bn * 4
    return pl.pallas_call(
        _mm_swiglu_kernel,
        out_shape=jax.ShapeDtypeStruct((M, F), BF16),
        grid=(M // bm, nb),
        in_specs=[pl.BlockSpec((bm, K), lambda i, j: (i, 0)),
                  pl.BlockSpec((K, bn), lambda i, j: (0, j)),
                  pl.BlockSpec((K, bn), lambda i, j: (0, j + nb))],
        out_specs=pl.BlockSpec((bm, bn), lambda i, j: (i, j)),
        compiler_params=_params(("parallel", "parallel"), vmem),
        name="matmul_swiglu",
    )(x, w_gu, w_gu)


def _conv_kernel(x_ref, w_ref, o_ref, *, T, heads_per_block, blocks_per_part):
    kind = pl.program_id(1) // blocks_per_part
    x = x_ref[...]
    w = w_ref[...]
    row = lax.broadcasted_iota(jnp.int32, x.shape, 0)
    acc = None
    for j in range(CONV_TAPS):
        s = CONV_TAPS - 1 - j
        xs = x if s == 0 else jnp.where(row >= s, pltpu.roll(x, s, 0), 0.0)
        term = xs * w[j:j + 1, :]
        acc = term if acc is None else acc + term
    y = _silu(acc)

    @pl.when(kind == 2)
    def _():
        o_ref[...] = y

    @pl.when(kind < 2)
    def _():
        scale = jnp.where(kind == 0, HEAD ** -0.5, 1.0).astype(F32)
        for h in range(heads_per_block):
            seg = y[:, h * HEAD:(h + 1) * HEAD]
            r = lax.rsqrt(jnp.sum(seg * seg, axis=-1, keepdims=True) + EPS)
            o_ref[:, h * HEAD:(h + 1) * HEAD] = seg * r * scale


def gdn_conv_prompt(proj, conv_w, D):
    B, T, _ = proj.shape
    cb = _divisor(D, 512, HEAD)
    vmem = 4 * T * cb * 4 + 6 * T * cb * 4
    return pl.pallas_call(
        functools.partial(_conv_kernel, T=T, heads_per_block=cb // HEAD, blocks_per_part=D // cb),
        out_shape=jax.ShapeDtypeStruct((B, T, 3 * D), F32),
        grid=(B, 3 * D // cb),
        in_specs=[pl.BlockSpec((None, T, cb), lambda b, j: (b, 0, j)),
                  pl.BlockSpec((CONV_TAPS, cb), lambda b, j: (0, j))],
        out_specs=pl.BlockSpec((None, T, cb), lambda b, j: (b, 0, j)),
        compiler_params=_params(("parallel", "parallel"), vmem),
        name="gdn_conv",
    )(proj, conv_w)


def _each_dot_sel(sel, xs):
    sel3 = jnp.concatenate([sel, sel, sel], axis=1).astype(BF16)
    stacked = [jnp.concatenate(_split3(x), axis=0) for x in xs]
    return [_dot(sel3, s) for s in stacked]


def _each_dot_x3(as_, bs):
    lhs, rhs = [], []
    for a, b in zip(as_, bs):
        ah = a.astype(BF16).astype(F32)
        lhs.append(jnp.concatenate([ah, ah, a - ah], axis=1).astype(BF16))
        bh, bl = _split2(b)
        rhs.append(jnp.concatenate([bh, bl, bh], axis=0))
    return [_dot(l, r) for l, r in zip(lhs, rhs)]


def _each_neumann_inverse(bms, L):
    eye = (lax.broadcasted_iota(jnp.int32, (L, L), 0) ==
           lax.broadcasted_iota(jnp.int32, (L, L), 1)).astype(F32)
    ps = [-bm for bm in bms]
    ts = [eye + p for p in ps]
    for _ in range(int(math.log2(L)) - 1):
        ps = _each_dot_x3(ps, ps)
        ts = [t + d for t, d in zip(ts, _each_dot_x3(ts, ps))]
    return ts


def _gdn_kernel(alog_ref, dtb_ref, q_ref, k_ref, v_ref, z_ref, ab_ref, onorm_ref,
                o_ref, s_out_ref,
                g_scr, b_scr, w_scr, u_scr, eg_scr, kt_scr, qk_scr, qkm_scr, s_scr,
                *, T, H, hb, n_meta, chunk, prep_unroll):
    n_chunks = (T - n_meta) // chunk
    heads = [(hh, pl.program_id(1) * hb + hh, slice(hh * HEAD, (hh + 1) * HEAD)) for hh in range(hb)]

    ab_parts = _split3(ab_ref[...])
    sel_row = lax.broadcasted_iota(jnp.int32, (2 * H, 2 * HEAD), 0)
    sel_col = lax.broadcasted_iota(jnp.int32, (2 * H, 2 * HEAD), 1)
    for _, h, lanes in heads:
        sel = (sel_row == jnp.where(sel_col < HEAD, h, h + H)).astype(BF16)
        ab_col = (_dot(ab_parts[0], sel) + _dot(ab_parts[1], sel)) + _dot(ab_parts[2], sel)
        decay_rate = jnp.exp(jnp.full((1, HEAD), alog_ref[h], F32))
        g_scr[:, lanes] = -decay_rate * _softplus(ab_col[:, :HEAD] + dtb_ref[h])
        b_scr[:, lanes] = _sigmoid(ab_col[:, HEAD:])

    def prep(offs, L, qk_ref):
        ri = lax.broadcasted_iota(jnp.int32, (L, L), 0)
        ci = lax.broadcasted_iota(jnp.int32, (L, L), 1)
        tri = ri >= ci
        strict = ri > ci
        eye = ri == ci
        jobs = [(pl.ds(off, L), pl.ds(qk_off, L), hh, lanes)
                for off, qk_off in offs for hh, _, lanes in heads]
        ks = [k_ref[rows, lanes] for rows, _, _, lanes in jobs]
        betas = [b_scr[rows, lanes] for rows, _, _, lanes in jobs]
        gcs = _each_dot_sel(tri.astype(F32), [g_scr[rows, lanes] for rows, _, _, lanes in jobs])
        gc_is = [gc[:, :L] for gc in gcs]
        gc_js = _each_dot_sel(jnp.ones((L, L), F32), [jnp.where(eye, g, 0.0) for g in gc_is])
        decays = [jnp.where(tri, jnp.exp(jnp.where(tri, gi - gj, 0.0)), 0.0) for gi, gj in zip(gc_is, gc_js)]
        kbs = [k.astype(BF16) for k in ks]
        kks = [_dot_nt(kb, kb) for kb in kbs]
        bms = [jnp.where(strict, kk * d, 0.0) * beta[:, :L] for kk, d, beta in zip(kks, decays, betas)]
        ts = _each_neumann_inverse(bms, L)
        egs = [jnp.exp(gc) for gc in gcs]
        wus = _each_dot_x3(ts, [jnp.concatenate([beta * eg * k, beta * v_ref[rows, lanes]], axis=1)
                                for beta, eg, k, (rows, _, _, lanes) in zip(betas, egs, ks, jobs)])
        ws = [wu[:, :HEAD] for wu in wus]
        us = [wu[:, HEAD:] for wu in wus]
        qks = [_dot_nt(q_ref[rows, lanes].astype(BF16), kb) * d
               for (rows, _, _, lanes), kb, d in zip(jobs, kbs, decays)]
        for (rows, qk_rows, hh, lanes), w, u, eg, k, gc, qk in zip(jobs, ws, us, egs, ks, gcs, qks):
            w_scr[rows, lanes] = w
            u_scr[rows, lanes] = u
            eg_scr[rows, lanes] = eg
            kt_scr[rows, lanes] = k * jnp.exp(gc[L - 1:L, :] - gc)
            qk_ref[hh, qk_rows, :] = qk

    def apply(off, L, qk_ref, qk_off):
        rows = pl.ds(off, L)
        ss = [s_scr[hh] for hh, _, _ in heads]
        sbs = [s.astype(BF16) for s in ss]
        us = [u_scr[rows, lanes] - _dot(w_scr[rows, lanes].astype(BF16), sb)
              for (_, _, lanes), sb in zip(heads, sbs)]
        ubs = [u.astype(BF16) for u in us]
        qss = [_dot(q_ref[rows, lanes].astype(BF16), sb) for (_, _, lanes), sb in zip(heads, sbs)]
        kus = [_dot_tn(kt_scr[rows, lanes].astype(BF16), ub) for (_, _, lanes), ub in zip(heads, ubs)]
        for (hh, _, lanes), s, ku in zip(heads, ss, kus):
            eg_last = eg_scr[pl.ds(off + L - 8, 8), lanes][7:8, :]
            s_scr[hh] = eg_last * s + ku
        for (hh, _, lanes), qs, ub in zip(heads, qss, ubs):
            o = eg_scr[rows, lanes] * qs + _dot(qk_ref[hh, pl.ds(qk_off, L), :].astype(BF16), ub)
            on = o * lax.rsqrt(jnp.mean(o * o, axis=-1, keepdims=True) + EPS) * onorm_ref[...]
            o_ref[rows, lanes] = (on * _silu(z_ref[rows, lanes])).astype(o_ref.dtype)

    def chunk_off(c):
        return pl.multiple_of(n_meta + c * chunk, BF16_SUBLANES)

    def chunk_qk_off(c):
        return pl.multiple_of(c * chunk, chunk)

    prep([(0, 0)], n_meta, qkm_scr)

    def prep_body(i, carry):
        cs = [i * prep_unroll + j for j in range(prep_unroll)]
        prep([(chunk_off(c), chunk_qk_off(c)) for c in cs], chunk, qk_scr)
        return carry

    lax.fori_loop(0, n_chunks // prep_unroll, prep_body, 0)

    s_scr[...] = jnp.zeros_like(s_scr)
    apply(0, n_meta, qkm_scr, 0)

    def apply_body(c, carry):
        apply(chunk_off(c), chunk, qk_scr, chunk_qk_off(c))
        return carry

    lax.fori_loop(0, n_chunks, apply_body, 0)
    s_out_ref[...] = s_scr[...]


def gdn_prompt(qkv_c, proj, ab, a_log, dt_bias, onorm, D):
    B, T, _ = qkv_c.shape
    H = D // HEAD
    assert (T - N_META_TOK) % GDN_CHUNK == 0
    n_chunks = (T - N_META_TOK) // GDN_CHUNK
    hb = 2 if H % 2 == 0 else 1
    prep_unroll = _divisor(n_chunks, 8, 1)
    W = hb * HEAD
    col = lambda part: pl.BlockSpec((None, T, W), lambda b, g: (b, 0, part * (H // hb) + g))
    smem = pl.BlockSpec(memory_space=pltpu.SMEM)
    vmem = (8 + 6) * T * W * 4 + 2 * T * LANES * 4 + 2 * T * W * 2 + hb * n_chunks * GDN_CHUNK * LANES * 4
    return pl.pallas_call(
        functools.partial(_gdn_kernel, T=T, H=H, hb=hb, n_meta=N_META_TOK, chunk=GDN_CHUNK,
                          prep_unroll=prep_unroll),
        out_shape=[jax.ShapeDtypeStruct((B, T, D), BF16),
                   jax.ShapeDtypeStruct((B, H, HEAD, HEAD), F32)],
        grid=(B, H // hb),
        in_specs=[smem, smem, col(0), col(1), col(2), col(3),
                  pl.BlockSpec((None, T, 2 * H), lambda b, g: (b, 0, 0)),
                  pl.BlockSpec((1, HEAD), lambda b, g: (0, 0))],
        out_specs=[pl.BlockSpec((None, T, W), lambda b, g: (b, 0, g)),
                   pl.BlockSpec((None, hb, HEAD, HEAD), lambda b, g: (b, g, 0, 0))],
        scratch_shapes=[pltpu.VMEM((T, W), F32)] * 6 + [
            pltpu.VMEM((hb, n_chunks * GDN_CHUNK, GDN_CHUNK), F32),
            pltpu.VMEM((hb, N_META_TOK, N_META_TOK), F32),
            pltpu.VMEM((hb, HEAD, HEAD), F32)],
        compiler_params=_params(("parallel", "parallel"), vmem),
        name="gdn_chunk",
    )(a_log.astype(F32), dt_bias.astype(F32), qkv_c, qkv_c, qkv_c, proj, ab, onorm.reshape(1, HEAD).astype(F32))


def _gdn_step_kernel(u_ref, cs_ref, w_ref, ab_ref, alog_ref, dtb_ref, onorm_ref, s_ref,
                     o_ref, s_out_ref, q_scr, k_scr, v_scr, g_scr, b_scr, o_scr, *, H):
    u_all = u_ref[...]
    u = u_all[:3 * H]
    xc = (cs_ref[0] * w_ref[0] + cs_ref[1] * w_ref[1] + cs_ref[2] * w_ref[2]) + u * w_ref[3]
    xc = _silu(xc)
    q = xc[:H]
    k = xc[H:2 * H]
    q_scr[...] = q * lax.rsqrt(jnp.sum(q * q, axis=-1, keepdims=True) + EPS) * (HEAD ** -0.5)
    k_scr[...] = k * lax.rsqrt(jnp.sum(k * k, axis=-1, keepdims=True) + EPS)
    v_scr[...] = xc[2 * H:]
    g_scr[...] = jnp.broadcast_to(-jnp.exp(alog_ref[...]) * _softplus(ab_ref[0] + dtb_ref[...]), (H, HEAD))
    b_scr[...] = jnp.broadcast_to(_sigmoid(ab_ref[1]), (H, HEAD))
    eye = (lax.broadcasted_iota(jnp.int32, (HEAD, HEAD), 0) ==
           lax.broadcasted_iota(jnp.int32, (HEAD, HEAD), 1)).astype(F32)

    def head(h, carry):
        row = pl.ds(h, 1)
        k_row = k_scr[row, :]
        s = s_ref[h] * jnp.exp(g_scr[row, :])
        upd = b_scr[row, :] * (v_scr[row, :] - _dot(k_row, s, HI))
        k_col = _dot_nt(eye, jnp.broadcast_to(k_row, (HEAD, HEAD)), HI)
        s = s + k_col * upd
        s_out_ref[h] = s
        o_scr[row, :] = _dot(q_scr[row, :], s, HI)
        return carry

    lax.fori_loop(0, H, head, 0)
    o = o_scr[...]
    on = o * lax.rsqrt(jnp.mean(o * o, axis=-1, keepdims=True) + EPS) * onorm_ref[...]
    o_ref[...] = (on * _silu(u_all[3 * H:])).astype(o_ref.dtype)


def gdn_sample(proj, ab, state_rec, state_conv, conv_w, a_log, dt_bias, onorm, D):
    Bd = proj.shape[0]
    H = D // HEAD
    col = lambda x: x.astype(F32).reshape(H, 1)
    vmem = 2 * (4 * H * HEAD * 4 + 3 * 3 * H * HEAD * 4 + 2 * H * HEAD * HEAD * 4) + 4 * 3 * H * HEAD * 4
    return pl.pallas_call(
        functools.partial(_gdn_step_kernel, H=H),
        out_shape=[jax.ShapeDtypeStruct((Bd, H, HEAD), BF16),
                   jax.ShapeDtypeStruct((Bd, H, HEAD, HEAD), F32)],
        grid=(Bd,),
        in_specs=[pl.BlockSpec((None, 4 * H, HEAD), lambda b: (b, 0, 0)),
                  pl.BlockSpec((None, CONV_TAPS - 1, 3 * H, HEAD), lambda b: (b, 0, 0, 0)),
                  pl.BlockSpec((CONV_TAPS, 3 * H, HEAD), lambda b: (0, 0, 0)),
                  pl.BlockSpec((None, 2, H, 1), lambda b: (b, 0, 0, 0)),
                  pl.BlockSpec((H, 1), lambda b: (0, 0)),
                  pl.BlockSpec((H, 1), lambda b: (0, 0)),
                  pl.BlockSpec((1, HEAD), lambda b: (0, 0)),
                  pl.BlockSpec((None, H, HEAD, HEAD), lambda b: (b, 0, 0, 0))],
        out_specs=[pl.BlockSpec((None, H, HEAD), lambda b: (b, 0, 0)),
                   pl.BlockSpec((None, H, HEAD, HEAD), lambda b: (b, 0, 0, 0))],
        scratch_shapes=[pltpu.VMEM((H, HEAD), F32)] * 6,
        compiler_params=_params(("parallel",), vmem),
        name="gdn_step",
    )(proj.reshape(Bd, 4 * H, HEAD), state_conv.reshape(Bd, CONV_TAPS - 1, 3 * H, HEAD),
      conv_w.reshape(CONV_TAPS, 3 * H, HEAD), ab.reshape(Bd, 2, H, 1), col(a_log), col(dt_bias),
      onorm.reshape(1, HEAD).astype(F32), state_rec)


def _logf_kernel(fl_ref, bf_ref, lf_ref, cum_ref, *, T):
    carry = jnp.zeros((1, fl_ref.shape[-1]), F32)
    for r0 in range(0, T, LANES):
        n = min(LANES, T - r0)
        lf = _log_sigmoid(fl_ref[r0:r0 + n, :] + bf_ref[...])
        lf_ref[r0:r0 + n, :] = lf
        tri = (lax.broadcasted_iota(jnp.int32, (n, n), 0) >=
               lax.broadcasted_iota(jnp.int32, (n, n), 1)).astype(F32)
        cum = _dot(tri, lf, HI) + carry
        cum_ref[r0:r0 + n, :] = cum
        carry = cum[n - 1:n, :]


def fox_logf(fl, b_f):
    B, T, H = fl.shape
    spec = pl.BlockSpec((None, T, H), lambda b: (b, 0, 0))
    return pl.pallas_call(
        functools.partial(_logf_kernel, T=T),
        out_shape=[jax.ShapeDtypeStruct((B, T, H), F32)] * 2,
        grid=(B,),
        in_specs=[spec, pl.BlockSpec((1, H), lambda b: (0, 0))],
        out_specs=[spec, spec],
        compiler_params=_params(("parallel",), 6 * T * LANES * 4),
        name="fox_logf",
    )(fl, b_f.reshape(1, H).astype(F32))


def _fox_kernel(q_ref, k_ref, v_ref, cum_ref, o_ref, kb_scr, vb_scr, cq_scr, ckm_scr, ckt_scr,
                *, T, H, bq):
    h = pl.program_id(1)
    t_main = (T // bq) * bq
    tail = T - t_main
    scale = HEAD ** -0.5
    kb_scr[...] = k_ref[...].astype(BF16)
    vb_scr[...] = v_ref[...].astype(BF16)
    cum = cum_ref[...]
    cq_scr[...] = _dot_sel_r(cum, (lax.broadcasted_iota(jnp.int32, (H, HEAD), 0) == h).astype(BF16))
    pick = (lax.broadcasted_iota(jnp.int32, (8, H), 1) == h).astype(BF16)
    ckm_scr[...] = _dot_sel(pick, cum[:t_main], _dot_nt)
    if tail:
        ckt_scr[...] = _dot_sel(pick, cum[t_main:], _dot_nt)

    def attend(r0, nq, ck_diag):
        rows = slice(r0, r0 + nq)
        qb = q_ref[rows, :].astype(BF16)
        cq = cq_scr[rows, 0:1]
        causal = (lax.broadcasted_iota(jnp.int32, (nq, nq), 0) >=
                  lax.broadcasted_iota(jnp.int32, (nq, nq), 1))
        s_d = jnp.where(causal, _dot_nt(qb, kb_scr[rows, :]) * scale + cq - ck_diag, -jnp.inf)
        m = jnp.max(s_d, axis=-1, keepdims=True)
        if r0:
            s_o = _dot_nt(qb, kb_scr[:r0, :]) * scale + cq - ckm_scr[0:1, :r0]
            m = jnp.maximum(m, jnp.max(s_o, axis=-1, keepdims=True))
        e_d = jnp.exp(s_d - m)
        l = jnp.sum(e_d, axis=-1, keepdims=True)
        o = _dot(e_d.astype(BF16), vb_scr[rows, :])
        if r0:
            e_o = jnp.exp(s_o - m)
            l = l + jnp.sum(e_o, axis=-1, keepdims=True)
            o = o + _dot(e_o.astype(BF16), vb_scr[:r0, :])
        o_ref[rows, :] = (o / l).astype(o_ref.dtype)

    for r0 in range(0, t_main, bq):
        attend(r0, bq, ckm_scr[0:1, r0:r0 + bq])
    if tail:
        attend(t_main, tail, ckt_scr[0:1, :])


def fox_prompt(q, k, v, cum):
    B, T, D = q.shape
    H = D // HEAD
    bq = 2 * LANES
    t_main = (T // bq) * bq
    tail = T - t_main
    assert tail % BF16_SUBLANES == 0
    head = pl.BlockSpec((None, T, HEAD), lambda b, h: (b, 0, h))
    vmem = 6 * T * HEAD * 4 + 2 * T * LANES * 4 + 2 * T * HEAD * 2 + 2 * T * HEAD * 2 + T * HEAD * 4 \
        + 8 * T * 4 + 6 * bq * t_main * 4
    return pl.pallas_call(
        functools.partial(_fox_kernel, T=T, H=H, bq=bq),
        out_shape=jax.ShapeDtypeStruct((B, T, D), BF16),
        grid=(B, H),
        in_specs=[head, head, head, pl.BlockSpec((None, T, H), lambda b, h: (b, 0, 0))],
        out_specs=head,
        scratch_shapes=[pltpu.VMEM((T, HEAD), BF16), pltpu.VMEM((T, HEAD), BF16),
                        pltpu.VMEM((T, HEAD), F32), pltpu.VMEM((8, t_main), F32),
                        pltpu.VMEM((8, max(tail, 8)), F32)],
        compiler_params=_params(("parallel", "parallel"), vmem),
        name="fox_prompt",
    )(q, k, v, cum)


def _page_suffix_kernel(lf_ref, within_ref, total_ref, *, pages):
    upper = (lax.broadcasted_iota(jnp.int32, (PAGE, PAGE), 0) <
             lax.broadcasted_iota(jnp.int32, (PAGE, PAGE), 1)).astype(BF16)
    ones = jnp.ones((PAGE, PAGE), BF16)
    for p in range(pages):
        hi, mid, lo = _split3(lf_ref[p])
        within_ref[p] = (_dot(upper, hi) + _dot(upper, mid)) + _dot(upper, lo)
        total_ref[p] = (_dot(ones, hi) + _dot(ones, mid)) + _dot(ones, lo)


def fox_page_suffix(cache_logf, layer):
    _, n_pool, _, H = cache_logf.shape
    pages = _divisor(n_pool, 16, 1)
    spec = pl.BlockSpec((pages, PAGE, H), lambda i: (i, 0, 0))
    return pl.pallas_call(
        functools.partial(_page_suffix_kernel, pages=pages),
        out_shape=[jax.ShapeDtypeStruct((n_pool, PAGE, H), F32)] * 2,
        grid=(n_pool // pages,),
        in_specs=[pl.BlockSpec((None, pages, PAGE, H), lambda i: (layer, i, 0, 0))],
        out_specs=[spec, spec],
        compiler_params=_params(("parallel",), 6 * pages * PAGE * LANES * 4),
        name="fox_page_suffix",
    )(cache_logf)


def _logsig_kernel(x_ref, b_ref, o_ref):
    o_ref[...] = _log_sigmoid(x_ref[...] + b_ref[...])


def fox_logf_step(fl, b_f):
    return pl.pallas_call(
        _logsig_kernel, out_shape=jax.ShapeDtypeStruct(fl.shape, F32), name="fox_logf_step",
    )(fl, b_f.reshape(1, -1).astype(F32))


def _fox_decode_kernel(pt_ref, *refs, n_pages, G, H):
    k_refs = refs[0:G]
    v_refs = refs[G:2 * G]
    w_refs = refs[2 * G:3 * G]
    t_refs = refs[3 * G:4 * G]
    q_ref, kn_ref, vn_ref, lfn_ref, lfc_ref, o_ref, m_scr, l_scr, acc_scr, carry_scr = refs[4 * G:]
    g = pl.program_id(1)
    scale = HEAD ** -0.5
    W = PAGE * H
    q = q_ref[...]
    qb = q.astype(BF16)
    lane = lax.broadcasted_iota(jnp.int32, (H, W), 1)
    lane_head = jnp.bitwise_and(lane, H - 1) if H & (H - 1) == 0 else lane % H
    own = lane_head == lax.broadcasted_iota(jnp.int32, (H, W), 0)

    @pl.when(g == 0)
    def _():
        m_scr[...] = jnp.full_like(m_scr, -jnp.inf)
        l_scr[...] = jnp.zeros_like(l_scr)
        acc_scr[...] = jnp.zeros_like(acc_scr)
        carry_scr[...] = jnp.zeros_like(carry_scr)

    def online_update(s, pv):
        m_old = m_scr[...]
        m_new = jnp.maximum(m_old, jnp.max(s, axis=-1, keepdims=True))
        alpha = jnp.exp(m_old - m_new)
        p = jnp.exp(s - m_new)
        l_scr[...] = l_scr[...] * alpha + jnp.sum(p, axis=-1, keepdims=True)
        acc_scr[...] = acc_scr[...] * alpha + pv(p)
        m_scr[...] = m_new

    for i in range(G):
        kb = k_refs[i][...].reshape(W, HEAD).astype(BF16)
        vb = v_refs[i][...].reshape(W, HEAD).astype(BF16)
        bias = lfn_ref[...] + (w_refs[i][...] + carry_scr[...])
        carry_scr[...] = carry_scr[...] + t_refs[i][...]
        s = jnp.where(own, _dot_nt(qb, kb) * scale + bias, -jnp.inf)
        online_update(s, lambda p: _dot(p.astype(BF16), vb))

    @pl.when(g == pl.num_programs(1) - 1)
    def _():
        cum_new = lfc_ref[...]
        s_new = jnp.sum(q * kn_ref[...], axis=-1, keepdims=True) * scale + (cum_new - cum_new)
        online_update(s_new, lambda p: p * vn_ref[...])
        o_ref[...] = (acc_scr[...] / l_scr[...]).astype(o_ref.dtype)


def fox_decode(q, k_new, v_new, fl_new, b_f, cache_k, cache_v, cache_logf, layer, page_table, D):
    Bd = q.shape[0]
    H = D // HEAD
    n_pages = page_table.shape[1]
    G = 4 if n_pages % 4 == 0 else 1
    W = PAGE * H
    lf_new = fox_logf_step(fl_new, b_f)
    within, total = fox_page_suffix(cache_logf, layer)
    within = within.reshape(-1, 1, W)
    total = total.reshape(-1, 1, W)

    def page_spec(block):
        def make(i):
            def index(b, g, pt):
                return block[0](pt[b, n_pages - 1 - (g * G + i)])
            return pl.BlockSpec(block[1], index)
        return [make(i) for i in range(G)]

    kv = ((lambda p: (layer, p, 0, 0, 0)), (None, None, PAGE, H, HEAD))
    flat = ((lambda p: (p, 0, 0)), (None, 1, W))
    per_seq = lambda shape: pl.BlockSpec((None,) + shape, lambda b, g, pt: (b, 0, 0))
    vmem = 2 * 2 * G * PAGE * D * 4 + 2 * G * PAGE * D * 2 + 4 * G * 8 * W * 4 + 10 * H * W * 4
    grid_spec = pltpu.PrefetchScalarGridSpec(
        num_scalar_prefetch=1,
        grid=(Bd, n_pages // G),
        in_specs=(page_spec(kv) + page_spec(kv) + page_spec(flat) + page_spec(flat)
                  + [per_seq((H, HEAD))] * 3 + [per_seq((1, W)), per_seq((H, 1))]),
        out_specs=per_seq((H, HEAD)),
        scratch_shapes=[pltpu.VMEM((H, 1), F32), pltpu.VMEM((H, 1), F32), pltpu.VMEM((H, HEAD), F32),
                        pltpu.VMEM((1, W), F32)],
    )
    heads = lambda x: x.reshape(Bd, H, HEAD)
    o = pl.pallas_call(
        functools.partial(_fox_decode_kernel, n_pages=n_pages, G=G, H=H),
        out_shape=jax.ShapeDtypeStruct((Bd, H, HEAD), BF16),
        grid_spec=grid_spec,
        compiler_params=_params(("parallel", "arbitrary"), vmem),
        name="fox_decode",
    )(page_table, *([cache_k] * G), *([cache_v] * G), *([within] * G), *([total] * G),
      heads(q), heads(k_new), heads(v_new), jnp.tile(lf_new, (1, PAGE)).reshape(Bd, 1, W),
      lf_new.reshape(Bd, H, 1))
    return o.reshape(Bd, D), lf_new


def _ffn(h, res, norm_w, w_gu, w_down):
    a = rmsnorm(h, norm_w, BF16)
    return matmul(matmul_swiglu(a, w_gu), w_down, residual=res)


def kernel(x_prompt, x_sample, cache_k, cache_v, cache_logf, page_table, state_rec, state_conv,
           meta_tokens, norm_mix, norm_ffn, norm_final, w_in_a, conv_w_a, a_log_a, dt_bias_a,
           onorm_a, w_out_a, w_in_b, b_f, w_out_b, w_gu, w_down):
    B, S, D = x_prompt.shape
    Bd, Q, _ = x_sample.shape
    assert Q == 1, "the sample group is a single-token step"
    H = D // HEAD
    T = S + N_META_TOK
    M = B * T
    meta = jnp.broadcast_to(meta_tokens.astype(x_prompt.dtype)[None], (B, N_META_TOK, D))
    hp = jnp.concatenate([meta, x_prompt], axis=1).reshape(M, D)
    hs = x_sample.reshape(Bd, D)
    bf = lambda w: w.astype(BF16)

    w_in = bf(w_in_a[0])
    w_ab = w_in[:, 4 * D:]
    w_out = bf(w_out_a[0])
    ap = rmsnorm(hp, norm_mix[0], BF16)
    a_s = rmsnorm(hs, norm_mix[0], BF16)
    proj_p = matmul(ap, w_in, cols=(0, 4 * D)).reshape(B, T, 4 * D)
    ab_p = matmul(ap, w_ab).reshape(B, T, 2 * H)
    proj_s = matmul(a_s, w_in, cols=(0, 4 * D))
    ab_s = matmul(a_s, w_ab)

    qkv_c = gdn_conv_prompt(proj_p, conv_w_a[0], D)
    on_p, rec_p = gdn_prompt(qkv_c, proj_p, ab_p, a_log_a[0], dt_bias_a[0], onorm_a[0], D)
    conv_p = proj_p[:, T - (CONV_TAPS - 1):, :3 * D]
    on_s, rec_s = gdn_sample(proj_s, ab_s, state_rec[0], state_conv[0], conv_w_a[0], a_log_a[0],
                             dt_bias_a[0], onorm_a[0], D)
    conv_s = jnp.concatenate([state_conv[0][:, 1:], proj_s[:, None, :3 * D].astype(state_conv.dtype)], axis=1)

    hp = matmul(on_p.reshape(M, D), w_out, residual=hp)
    hs = matmul(on_s.reshape(Bd, D), w_out, residual=hs)
    w_gu0, w_dn0 = bf(w_gu[0]), bf(w_down[0])
    hp = _ffn(hp, hp, norm_ffn[0], w_gu0, w_dn0)
    hs = _ffn(hs, hs, norm_ffn[0], w_gu0, w_dn0)

    w_in = bf(w_in_b[0])
    w_f = w_in[:, 3 * D:]
    w_out = bf(w_out_b[0])
    ap = rmsnorm(hp, norm_mix[1], BF16)
    a_s = rmsnorm(hs, norm_mix[1], BF16)
    q_p, k_p, v_p = (matmul(ap, w_in, cols=(part * D, D)).reshape(B, T, D) for part in range(3))
    fl_p = matmul(ap, w_f).reshape(B, T, H)
    qkv_s = matmul(a_s, w_in, cols=(0, 3 * D))
    fl_s = matmul(a_s, w_f)

    lf_p, cum_p = fox_logf(fl_p, b_f[0])
    o_p = fox_prompt(q_p, k_p, v_p, cum_p)
    o_s, lf_s = fox_decode(qkv_s[:, :D], qkv_s[:, D:2 * D], qkv_s[:, 2 * D:], fl_s, b_f[0],
                           cache_k, cache_v, cache_logf, 0, page_table, D)

    hp = matmul(o_p.reshape(M, D), w_out, residual=hp)
    hs = matmul(o_s, w_out, residual=hs)
    w_gu1, w_dn1 = bf(w_gu[1]), bf(w_down[1])
    hp = _ffn(hp, hp, norm_ffn[1], w_gu1, w_dn1)
    hs = _ffn(hs, hs, norm_ffn[1], w_gu1, w_dn1)

    y_prompt = rmsnorm(hp, norm_final, F32).reshape(B, T, D)[:, N_META_TOK:]
    y_sample = rmsnorm(hs, norm_final, F32).reshape(Bd, 1, D)

    k_prompt = k_p.reshape(1, B, T, H, HEAD)
    v_prompt = v_p.reshape(1, B, T, H, HEAD)
    logf_prompt = lf_p.astype(cache_logf.dtype)[None]
    k_sample = qkv_s[:, D:2 * D].reshape(1, Bd, 1, H, HEAD)
    v_sample = qkv_s[:, 2 * D:].reshape(1, Bd, 1, H, HEAD)
    logf_sample = lf_s.astype(cache_logf.dtype).reshape(1, Bd, 1, H)
    return (y_prompt, y_sample, k_prompt, v_prompt, logf_prompt, k_sample, v_sample, logf_sample,
            rec_p[None], conv_p[None], rec_s.astype(state_rec.dtype)[None], conv_s[None])
```

```python
import functools
import math

import jax
import jax.numpy as jnp
from jax import lax
from jax.experimental import pallas as pl
from jax.experimental.pallas import tpu as pltpu

F32 = jnp.float32
BF16 = jnp.bfloat16
HI = lax.Precision.HIGHEST

HEAD = 128
N_META_TOK = 16
CONV_TAPS = 4
GDN_CHUNK = 64
PAGE = 128
EPS = 1e-6

VMEM_BYTES_V7X = 64 << 20
VMEM_LIMIT_MAX = VMEM_BYTES_V7X - (8 << 20)
BF16_SUBLANES = 16
LANES = 128


def _dot(a, b, precision=None):
    return jnp.dot(a, b, preferred_element_type=F32, precision=precision)


def _dot_nt(a, b, precision=None):
    return lax.dot_general(a, b, (((1,), (1,)), ((), ())), preferred_element_type=F32,
                           precision=precision)


def _dot_tn(a, b, precision=None):
    return lax.dot_general(a, b, (((0,), (0,)), ((), ())), preferred_element_type=F32,
                           precision=precision)


def _split2(x):
    hi = x.astype(BF16)
    return hi, (x - hi.astype(F32)).astype(BF16)


def _split3(x):
    hi = x.astype(BF16)
    r = x - hi.astype(F32)
    mid = r.astype(BF16)
    return hi, mid, (r - mid.astype(F32)).astype(BF16)


def _dot_sel(sel, x, dot=_dot):
    hi, mid, lo = _split3(x)
    return (dot(sel, hi) + dot(sel, mid)) + dot(sel, lo)


def _dot_sel_r(x, sel, dot=_dot):
    hi, mid, lo = _split3(x)
    return (dot(hi, sel) + dot(mid, sel)) + dot(lo, sel)


def _dot_x3(a, b):
    ah, al = _split2(a)
    bh, bl = _split2(b)
    return _dot(ah, bh) + (_dot(ah, bl) + _dot(al, bh))


def _sigmoid(x):
    return 1.0 / (1.0 + jnp.exp(-x))


def _silu(x):
    return x * _sigmoid(x)


def _softplus(x):
    return jnp.maximum(x, 0.0) + jnp.log1p(jnp.exp(-jnp.abs(x)))


def _log_sigmoid(x):
    return -_softplus(-x)


def _divisor(n, cap, align):
    best = None
    for d in range(align, min(n, cap) + 1, align):
        if n % d == 0:
            best = d
    return n if best is None else best


def _params(semantics, vmem_bytes):
    limit = int(min(max(vmem_bytes * 5 // 4 + (4 << 20), 32 << 20), VMEM_LIMIT_MAX))
    return pltpu.CompilerParams(dimension_semantics=semantics, vmem_limit_bytes=limit)


def _rmsnorm_kernel(x_ref, w_ref, o_ref):
    x = x_ref[...]
    y = x * lax.rsqrt(jnp.mean(x * x, axis=-1, keepdims=True) + EPS)
    o_ref[...] = (y * w_ref[...]).astype(o_ref.dtype)


def rmsnorm(x, w, out_dtype):
    M, D = x.shape
    bm = _divisor(M, max(BF16_SUBLANES, (3 << 20) // (4 * D)), BF16_SUBLANES)
    vmem = 2 * bm * D * (4 + jnp.dtype(out_dtype).itemsize)
    return pl.pallas_call(
        _rmsnorm_kernel,
        out_shape=jax.ShapeDtypeStruct((M, D), out_dtype),
        grid=(M // bm,),
        in_specs=[pl.BlockSpec((bm, D), lambda i: (i, 0)),
                  pl.BlockSpec((1, D), lambda i: (0, 0))],
        out_specs=pl.BlockSpec((bm, D), lambda i: (i, 0)),
        compiler_params=_params(("parallel",), vmem),
        name="rmsnorm",
    )(x, w.reshape(1, D).astype(F32))


def _mm_kernel(x_ref, w_ref, o_ref):
    o_ref[...] = _dot(x_ref[...], w_ref[...].astype(BF16)).astype(o_ref.dtype)


def _mm_res_kernel(x_ref, w_ref, r_ref, o_ref):
    o_ref[...] = (r_ref[...] + _dot(x_ref[...], w_ref[...].astype(BF16))).astype(o_ref.dtype)


def _mm_swiglu_kernel(x_ref, wg_ref, wu_ref, o_ref):
    x = x_ref[...]
    g = _dot(x, wg_ref[...].astype(BF16))
    u = _dot(x, wu_ref[...].astype(BF16))
    o_ref[...] = (_silu(g) * u).astype(o_ref.dtype)


MM_VMEM_BUDGET = 48 << 20


def _mm_vmem(bm, bn, K, w_bytes, n_weights, out_bytes, residual):
    cast = n_weights * K * bn * 2 if w_bytes != 2 else 0
    return (bm * K * 2 + 2 * n_weights * K * bn * w_bytes + cast + 2 * bm * bn * out_bytes
            + (2 + n_weights) * bm * bn * 4 + (2 * bm * bn * 4 if residual else 0))


def _mm_tiles(M, K, N, w_bytes, n_weights=1, out_bytes=4, residual=False):
    bm = _divisor(M, max(BF16_SUBLANES, (16 << 20) // (2 * K)), BF16_SUBLANES)
    if N % LANES:
        return bm, N
    bn = LANES
    for cand in (4 * LANES, 2 * LANES):
        if N % cand == 0 and _mm_vmem(bm, cand, K, w_bytes, n_weights, out_bytes, residual) <= MM_VMEM_BUDGET:
            bn = cand
            break
    return bm, bn


def _x_spec(bm, K):
    return pl.BlockSpec((bm, K), lambda i, j: (i, 0), pipeline_mode=pl.Buffered(1))


def matmul(x, w, layer=0, residual=None, out_dtype=F32, cols=None):
    M, K = x.shape
    col0, N = (0, w.shape[2]) if cols is None else cols
    w_bytes = jnp.dtype(w.dtype).itemsize
    out_bytes = jnp.dtype(out_dtype).itemsize
    bm, bn = _mm_tiles(M, K, N, w_bytes, out_bytes=out_bytes, residual=residual is not None)
    assert col0 % bn == 0 and col0 + N <= w.shape[2]
    j0 = col0 // bn
    in_specs = [_x_spec(bm, K), pl.BlockSpec((None, K, bn), lambda i, j: (layer, 0, j + j0))]
    args = [x, w]
    body = _mm_kernel
    if residual is not None:
        in_specs.append(pl.BlockSpec((bm, bn), lambda i, j: (i, j)))
        args.append(residual)
        body = _mm_res_kernel
    return pl.pallas_call(
        body,
        out_shape=jax.ShapeDtypeStruct((M, N), out_dtype),
        grid=(M // bm, N // bn),
        in_specs=in_specs,
        out_specs=pl.BlockSpec((bm, bn), lambda i, j: (i, j)),
        compiler_params=_params(("parallel", "parallel"),
                                _mm_vmem(bm, bn, K, w_bytes, 1, out_bytes, residual is not None)),
        name="matmul",
    )(*args)


def matmul_swiglu(x, w_gu, layer):
    M, K = x.shape
    F = w_gu.shape[2] // 2
    w_bytes = jnp.dtype(w_gu.dtype).itemsize
    bm, bn = _mm_tiles(M, K, F, w_bytes, n_weights=2, out_bytes=2)
    nb = F // bn
    return pl.pallas_call(
        _mm_swiglu_kernel,
        out_shape=jax.ShapeDtypeStruct((M, F), BF16),
        grid=(M // bm, nb),
        in_specs=[_x_spec(bm, K),
                  pl.BlockSpec((None, K, bn), lambda i, j: (layer, 0, j)),
                  pl.BlockSpec((None, K, bn), lambda i, j: (layer, 0, j + nb))],
        out_specs=pl.BlockSpec((bm, bn), lambda i, j: (i, j)),
        compiler_params=_params(("parallel", "parallel"), _mm_vmem(bm, bn, K, w_bytes, 2, 2, False)),
        name="matmul_swiglu",
    )(x, w_gu, w_gu)


def _conv_kernel(x_ref, w_ref, o_ref, *, T, rows, heads_per_block, blocks_per_part):
    kind = pl.program_id(1) // blocks_per_part
    w = w_ref[...]
    scale = jnp.where(kind == 0, HEAD ** -0.5, 1.0).astype(F32)
    halo = 8

    def conv_silu(i):
        r0 = pl.multiple_of(i * rows, rows)
        cur = x_ref[pl.ds(r0, rows), :]
        prev = x_ref[pl.ds(pl.multiple_of(jnp.maximum(r0 - halo, 0), halo), halo), :]
        xx = jnp.concatenate([jnp.where(i > 0, prev, 0.0), cur], axis=0)
        acc = None
        for j in range(CONV_TAPS):
            s = CONV_TAPS - 1 - j
            xs = cur if s == 0 else pltpu.roll(xx, s, 0)[halo:, :]
            term = xs * w[j:j + 1, :]
            acc = term if acc is None else acc + term
        return r0, _silu(acc)

    def qk_chunk(i, carry):
        r0, y = conv_silu(i)
        for h in range(heads_per_block):
            seg = y[:, h * HEAD:(h + 1) * HEAD]
            r = lax.rsqrt(jnp.sum(seg * seg, axis=-1, keepdims=True) + EPS)
            o_ref[pl.ds(r0, rows), h * HEAD:(h + 1) * HEAD] = seg * r * scale
        return carry

    def v_chunk(i, carry):
        r0, y = conv_silu(i)
        o_ref[pl.ds(r0, rows), :] = y
        return carry

    n = T // rows
    unroll = 2

    @pl.when(kind < 2)
    def _():
        lax.fori_loop(0, n, qk_chunk, 0, unroll=unroll)

    @pl.when(kind == 2)
    def _():
        lax.fori_loop(0, n, v_chunk, 0, unroll=unroll)


def gdn_conv_prompt(proj, conv_w, D):
    B, T, _ = proj.shape
    cb = _divisor(D, 512, HEAD)
    rows = _divisor(T, 64, 8)
    vmem = 4 * T * cb * 4
    return pl.pallas_call(
        functools.partial(_conv_kernel, T=T, rows=rows, heads_per_block=cb // HEAD,
                          blocks_per_part=D // cb),
        out_shape=jax.ShapeDtypeStruct((B, T, 3 * D), F32),
        grid=(B, 3 * D // cb),
        in_specs=[pl.BlockSpec((None, T, cb), lambda b, j: (b, 0, j)),
                  pl.BlockSpec((CONV_TAPS, cb), lambda b, j: (0, j))],
        out_specs=pl.BlockSpec((None, T, cb), lambda b, j: (b, 0, j)),
        compiler_params=_params(("parallel", "parallel"), vmem),
        name="gdn_conv",
    )(proj, conv_w)


def _each_dot_sel(sel, xs):
    sel3 = jnp.concatenate([sel, sel, sel], axis=1).astype(BF16)
    stacked = [jnp.concatenate(_split3(x), axis=0) for x in xs]
    return [_dot(sel3, s) for s in stacked]


def _each_dot_x3(as_, bs):
    lhs, rhs = [], []
    for a, b in zip(as_, bs):
        ah = a.astype(BF16).astype(F32)
        lhs.append(jnp.concatenate([ah, ah, a - ah], axis=1).astype(BF16))
        bh, bl = _split2(b)
        rhs.append(jnp.concatenate([bh, bl, bh], axis=0))
    return [_dot(l, r) for l, r in zip(lhs, rhs)]


def _each_neumann_inverse(bms, L):
    eye = (lax.broadcasted_iota(jnp.int32, (L, L), 0) ==
           lax.broadcasted_iota(jnp.int32, (L, L), 1)).astype(F32)
    ps = [-bm for bm in bms]
    ts = [eye + p for p in ps]
    for _ in range(int(math.log2(L)) - 1):
        ps = _each_dot_x3(ps, ps)
        ts = [t + d for t, d in zip(ts, _each_dot_x3(ts, ps))]
    return ts


def _gdn_kernel(alog_ref, dtb_ref, q_ref, k_ref, v_ref, z_ref, ab_ref, onorm_ref,
                o_ref, s_out_ref,
                g_scr, b_scr, w_scr, u_scr, eg_scr, a_scr, c_scr, qk_scr, qkm_scr, s_scr,
                *, T, H, hb, n_meta, chunk, prep_unroll):
    n_chunks = (T - n_meta) // chunk
    heads = [(hh, pl.program_id(1) * hb + hh, slice(hh * HEAD, (hh + 1) * HEAD)) for hh in range(hb)]

    ab_parts = _split3(ab_ref[...])
    sel_row = lax.broadcasted_iota(jnp.int32, (2 * H, 2 * HEAD), 0)
    sel_col = lax.broadcasted_iota(jnp.int32, (2 * H, 2 * HEAD), 1)
    for _, h, lanes in heads:
        sel = (sel_row == jnp.where(sel_col < HEAD, h, h + H)).astype(BF16)
        ab_col = (_dot(ab_parts[0], sel) + _dot(ab_parts[1], sel)) + _dot(ab_parts[2], sel)
        decay_rate = jnp.exp(jnp.full((1, HEAD), alog_ref[h], F32))
        g_scr[:, lanes] = -decay_rate * _softplus(ab_col[:, :HEAD] + dtb_ref[h])
        b_scr[:, lanes] = _sigmoid(ab_col[:, HEAD:])

    def prep(offs, L, qk_ref):
        ri = lax.broadcasted_iota(jnp.int32, (L, L), 0)
        ci = lax.broadcasted_iota(jnp.int32, (L, L), 1)
        tri = ri >= ci
        strict = ri > ci
        eye = ri == ci
        jobs = [(pl.ds(off, L), pl.ds(qk_off, L), hh, lanes) for off, qk_off, _ in offs for hh, _, lanes in heads]
        slots = [slot for _, _, slot in offs for _ in heads]
        ks = [k_ref[rows, lanes] for rows, _, _, lanes in jobs]
        betas = [b_scr[rows, lanes] for rows, _, _, lanes in jobs]
        gcs = _each_dot_sel(tri.astype(F32), [g_scr[rows, lanes] for rows, _, _, lanes in jobs])
        gc_is = [gc[:, :L] for gc in gcs]
        gc_js = _each_dot_sel(jnp.ones((L, L), F32), [jnp.where(eye, g, 0.0) for g in gc_is])
        decays = [jnp.where(tri, jnp.exp(jnp.where(tri, gi - gj, 0.0)), 0.0) for gi, gj in zip(gc_is, gc_js)]
        kbs = [k.astype(BF16) for k in ks]
        kks = [_dot_nt(kb, kb) for kb in kbs]
        bms = [jnp.where(strict, kk * d, 0.0) * beta[:, :L] for kk, d, beta in zip(kks, decays, betas)]
        ts = _each_neumann_inverse(bms, L)
        egs = [jnp.exp(gc) for gc in gcs]
        wus = _each_dot_x3(ts, [jnp.concatenate([beta * eg * k, beta * v_ref[rows, lanes]], axis=1)
                                for beta, eg, k, (rows, _, _, lanes) in zip(betas, egs, ks, jobs)])
        ws = [wu[:, :HEAD] for wu in wus]
        us = [wu[:, HEAD:] for wu in wus]
        qks = [_dot_nt(q_ref[rows, lanes].astype(BF16), kb) * d
               for (rows, _, _, lanes), kb, d in zip(jobs, kbs, decays)]
        kts = [(k * jnp.exp(gc[L - 1:L, :] - gc)).astype(BF16) for k, gc in zip(ks, gcs)]
        acs = [_dot_tn(kt, wu.astype(BF16)) for kt, wu in zip(kts, wus)]
        for (rows, qk_rows, hh, lanes), slot, w, u, eg, ac, qk in zip(jobs, slots, ws, us, egs, acs, qks):
            w_scr[rows, lanes] = w
            u_scr[rows, lanes] = u
            eg_scr[rows, lanes] = eg
            a_scr[hh, slot] = ac[:, :HEAD].astype(BF16)
            c_scr[hh, slot] = ac[:, HEAD:]
            qk_ref[hh, qk_rows, :] = qk

    def advance(off, L, slot):
        rows = pl.ds(off, L)
        ss = [s_scr[hh] for hh, _, _ in heads]
        lhs = [jnp.concatenate([a_scr[hh, slot], w_scr[rows, lanes].astype(BF16),
                                q_ref[rows, lanes].astype(BF16)], axis=0) for hh, _, lanes in heads]
        prods = [_dot(l, s.astype(BF16)) for l, s in zip(lhs, ss)]
        for (hh, _, lanes), s, p in zip(heads, ss, prods):
            eg_last = eg_scr[pl.ds(off + L - 8, 8), lanes][7:8, :]
            s_scr[hh] = (eg_last * s - p[:HEAD]) + c_scr[hh, slot]
        return tuple(x for (_, _, lanes), p in zip(heads, prods)
                     for x in (u_scr[rows, lanes] - p[HEAD:HEAD + L], p[HEAD + L:]))

    def emit(off, L, qk_ref, qk_off, carried):
        rows = pl.ds(off, L)
        for i, (hh, _, lanes) in enumerate(heads):
            u, qs = carried[2 * i], carried[2 * i + 1]
            o = eg_scr[rows, lanes] * qs + _dot(qk_ref[hh, pl.ds(qk_off, L), :].astype(BF16), u.astype(BF16))
            on = o * lax.rsqrt(jnp.mean(o * o, axis=-1, keepdims=True) + EPS) * onorm_ref[...]
            o_ref[rows, lanes] = (on * _silu(z_ref[rows, lanes])).astype(o_ref.dtype)

    def chunk_off(c):
        off = n_meta + c * chunk
        return off if isinstance(c, int) else pl.multiple_of(off, BF16_SUBLANES)

    def chunk_qk_off(c):
        return c * chunk if isinstance(c, int) else pl.multiple_of(c * chunk, chunk)

    prep([(0, 0, 0)], n_meta, qkm_scr)

    def prep_body(i, carry):
        cs = [i * prep_unroll + j for j in range(prep_unroll)]
        prep([(chunk_off(c), chunk_qk_off(c), c + 1) for c in cs], chunk, qk_scr)
        return carry

    lax.fori_loop(0, n_chunks // prep_unroll, prep_body, 0)

    s_scr[...] = jnp.zeros_like(s_scr)
    emit(0, n_meta, qkm_scr, 0, advance(0, n_meta, 0))

    def step(c, carried):
        new = advance(chunk_off(c), chunk, c + 1)
        emit(chunk_off(c - 1), chunk, qk_scr, chunk_qk_off(c - 1), carried)
        return new

    last = lax.fori_loop(1, n_chunks, step, advance(chunk_off(0), chunk, 1))
    emit(chunk_off(n_chunks - 1), chunk, qk_scr, chunk_qk_off(n_chunks - 1), last)
    s_out_ref[...] = s_scr[...]


def gdn_prompt(qkv_c, proj, ab, a_log, dt_bias, onorm, D):
    B, T, _ = qkv_c.shape
    H = D // HEAD
    assert (T - N_META_TOK) % GDN_CHUNK == 0
    n_chunks = (T - N_META_TOK) // GDN_CHUNK
    hb = 2 if H % 2 == 0 else 1
    prep_unroll = _divisor(n_chunks, 8, 1)
    W = hb * HEAD
    col = lambda part: pl.BlockSpec((None, T, W), lambda b, g: (b, 0, part * (H // hb) + g))
    smem = pl.BlockSpec(memory_space=pltpu.SMEM)
    vmem = ((8 + 5) * T * W * 4 + 2 * T * LANES * 4 + 2 * T * W * 2 + hb * n_chunks * GDN_CHUNK * LANES * 4
            + hb * (n_chunks + 1) * HEAD * HEAD * 6)
    return pl.pallas_call(
        functools.partial(_gdn_kernel, T=T, H=H, hb=hb, n_meta=N_META_TOK, chunk=GDN_CHUNK,
                          prep_unroll=prep_unroll),
        out_shape=[jax.ShapeDtypeStruct((B, T, D), BF16),
                   jax.ShapeDtypeStruct((B, H, HEAD, HEAD), F32)],
        grid=(B, H // hb),
        in_specs=[smem, smem, col(0), col(1), col(2), col(3),
                  pl.BlockSpec((None, T, 2 * H), lambda b, g: (b, 0, 0)),
                  pl.BlockSpec((1, HEAD), lambda b, g: (0, 0))],
        out_specs=[pl.BlockSpec((None, T, W), lambda b, g: (b, 0, g)),
                   pl.BlockSpec((None, hb, HEAD, HEAD), lambda b, g: (b, g, 0, 0))],
        scratch_shapes=[pltpu.VMEM((T, W), F32)] * 5 + [
            pltpu.VMEM((hb, n_chunks + 1, HEAD, HEAD), BF16),
            pltpu.VMEM((hb, n_chunks + 1, HEAD, HEAD), F32),
            pltpu.VMEM((hb, n_chunks * GDN_CHUNK, GDN_CHUNK), F32),
            pltpu.VMEM((hb, N_META_TOK, N_META_TOK), F32),
            pltpu.VMEM((hb, HEAD, HEAD), F32)],
        compiler_params=_params(("parallel", "parallel"), vmem),
        name="gdn_chunk",
    )(a_log.astype(F32), dt_bias.astype(F32), qkv_c, qkv_c, qkv_c, proj, ab, onorm.reshape(1, HEAD).astype(F32))


def _gdn_step_kernel(u_ref, cs_ref, w_ref, ab_ref, alog_ref, dtb_ref, onorm_ref, s_ref,
                     o_ref, s_out_ref, q_scr, k_scr, v_scr, g_scr, b_scr, o_scr, *, H):
    u_all = u_ref[...]
    u = u_all[:3 * H]
    xc = (cs_ref[0] * w_ref[0] + cs_ref[1] * w_ref[1] + cs_ref[2] * w_ref[2]) + u * w_ref[3]
    xc = _silu(xc)
    q = xc[:H]
    k = xc[H:2 * H]
    q_scr[...] = q * lax.rsqrt(jnp.sum(q * q, axis=-1, keepdims=True) + EPS) * (HEAD ** -0.5)
    k_scr[...] = k * lax.rsqrt(jnp.sum(k * k, axis=-1, keepdims=True) + EPS)
    v_scr[...] = xc[2 * H:]
    g_scr[...] = jnp.broadcast_to(-jnp.exp(alog_ref[...]) * _softplus(ab_ref[0] + dtb_ref[...]), (H, HEAD))
    b_scr[...] = jnp.broadcast_to(_sigmoid(ab_ref[1]), (H, HEAD))
    eye = (lax.broadcasted_iota(jnp.int32, (HEAD, HEAD), 0) ==
           lax.broadcasted_iota(jnp.int32, (HEAD, HEAD), 1)).astype(F32)

    def head(h, carry):
        row = pl.ds(h, 1)
        k_row = k_scr[row, :]
        s = s_ref[h] * jnp.exp(g_scr[row, :])
        upd = b_scr[row, :] * (v_scr[row, :] - _dot(k_row, s, HI))
        k_col = _dot_nt(eye, jnp.broadcast_to(k_row, (HEAD, HEAD)), HI)
        s = s + k_col * upd
        s_out_ref[h] = s
        o_scr[row, :] = _dot(q_scr[row, :], s, HI)
        return carry

    lax.fori_loop(0, H, head, 0)
    o = o_scr[...]
    on = o * lax.rsqrt(jnp.mean(o * o, axis=-1, keepdims=True) + EPS) * onorm_ref[...]
    o_ref[...] = (on * _silu(u_all[3 * H:])).astype(o_ref.dtype)


def gdn_sample(proj, ab, state_rec, state_conv, conv_w, a_log, dt_bias, onorm, D):
    Bd = proj.shape[0]
    H = D // HEAD
    col = lambda x: x.astype(F32).reshape(H, 1)
    vmem = 2 * (4 * H * HEAD * 4 + 3 * 3 * H * HEAD * 4 + 2 * H * HEAD * HEAD * 4) + 4 * 3 * H * HEAD * 4
    return pl.pallas_call(
        functools.partial(_gdn_step_kernel, H=H),
        out_shape=[jax.ShapeDtypeStruct((Bd, H, HEAD), BF16),
                   jax.ShapeDtypeStruct((Bd, H, HEAD, HEAD), F32)],
        grid=(Bd,),
        in_specs=[pl.BlockSpec((None, 4 * H, HEAD), lambda b: (b, 0, 0)),
                  pl.BlockSpec((None, CONV_TAPS - 1, 3 * H, HEAD), lambda b: (b, 0, 0, 0)),
                  pl.BlockSpec((CONV_TAPS, 3 * H, HEAD), lambda b: (0, 0, 0)),
                  pl.BlockSpec((None, 2, H, 1), lambda b: (b, 0, 0, 0)),
                  pl.BlockSpec((H, 1), lambda b: (0, 0)),
                  pl.BlockSpec((H, 1), lambda b: (0, 0)),
                  pl.BlockSpec((1, HEAD), lambda b: (0, 0)),
                  pl.BlockSpec((None, H, HEAD, HEAD), lambda b: (b, 0, 0, 0))],
        out_specs=[pl.BlockSpec((None, H, HEAD), lambda b: (b, 0, 0)),
                   pl.BlockSpec((None, H, HEAD, HEAD), lambda b: (b, 0, 0, 0))],
        scratch_shapes=[pltpu.VMEM((H, HEAD), F32)] * 6,
        compiler_params=_params(("parallel",), vmem),
        name="gdn_step",
    )(proj.reshape(Bd, 4 * H, HEAD), state_conv.reshape(Bd, CONV_TAPS - 1, 3 * H, HEAD),
      conv_w.reshape(CONV_TAPS, 3 * H, HEAD), ab.reshape(Bd, 2, H, 1), col(a_log), col(dt_bias),
      onorm.reshape(1, HEAD).astype(F32), state_rec)


def _logf_kernel(fl_ref, bf_ref, lf_ref, cum_ref, *, T):
    carry = jnp.zeros((1, fl_ref.shape[-1]), F32)
    for r0 in range(0, T, LANES):
        n = min(LANES, T - r0)
        lf = _log_sigmoid(fl_ref[r0:r0 + n, :] + bf_ref[...])
        lf_ref[r0:r0 + n, :] = lf
        tri = (lax.broadcasted_iota(jnp.int32, (n, n), 0) >=
               lax.broadcasted_iota(jnp.int32, (n, n), 1)).astype(F32)
        cum = _dot(tri, lf, HI) + carry
        cum_ref[r0:r0 + n, :] = cum
        carry = cum[n - 1:n, :]


def fox_logf(fl, b_f):
    B, T, H = fl.shape
    spec = pl.BlockSpec((None, T, H), lambda b: (b, 0, 0))
    return pl.pallas_call(
        functools.partial(_logf_kernel, T=T),
        out_shape=[jax.ShapeDtypeStruct((B, T, H), F32)] * 2,
        grid=(B,),
        in_specs=[spec, pl.BlockSpec((1, H), lambda b: (0, 0))],
        out_specs=[spec, spec],
        compiler_params=_params(("parallel",), 6 * T * LANES * 4),
        name="fox_logf",
    )(fl, b_f.reshape(1, H).astype(F32))


def _fox_kernel(q_ref, k_ref, v_ref, cum_ref, o_ref, kb_scr, vb_scr, cq_scr, ckm_scr, ckt_scr,
                *, T, H, bq):
    h = pl.program_id(1)
    t_main = (T // bq) * bq
    tail = T - t_main
    scale = HEAD ** -0.5
    kb_scr[...] = k_ref[...].astype(BF16)
    vb_scr[...] = v_ref[...].astype(BF16)
    cum = cum_ref[...]
    cq_scr[...] = _dot_sel_r(cum, (lax.broadcasted_iota(jnp.int32, (H, HEAD), 0) == h).astype(BF16))
    pick = (lax.broadcasted_iota(jnp.int32, (8, H), 1) == h).astype(BF16)
    ckm_scr[...] = _dot_sel(pick, cum[:t_main], _dot_nt)
    if tail:
        ckt_scr[...] = _dot_sel(pick, cum[t_main:], _dot_nt)

    def attend(r0, nq, ck_diag):
        rows = slice(r0, r0 + nq)
        qb = q_ref[rows, :].astype(BF16)
        cq = cq_scr[rows, 0:1]
        causal = (lax.broadcasted_iota(jnp.int32, (nq, nq), 0) >=
                  lax.broadcasted_iota(jnp.int32, (nq, nq), 1))
        s_d = jnp.where(causal, _dot_nt(qb, kb_scr[rows, :]) * scale + cq - ck_diag, -jnp.inf)
        m = jnp.max(s_d, axis=-1, keepdims=True)
        if r0:
            s_o = _dot_nt(qb, kb_scr[:r0, :]) * scale + cq - ckm_scr[0:1, :r0]
            m = jnp.maximum(m, jnp.max(s_o, axis=-1, keepdims=True))
        e_d = jnp.exp(s_d - m)
        l = jnp.sum(e_d, axis=-1, keepdims=True)
        o = _dot(e_d.astype(BF16), vb_scr[rows, :])
        if r0:
            e_o = jnp.exp(s_o - m)
            l = l + jnp.sum(e_o, axis=-1, keepdims=True)
            o = o + _dot(e_o.astype(BF16), vb_scr[:r0, :])
        o_ref[rows, :] = (o / l).astype(o_ref.dtype)

    for r0 in range(0, t_main, bq):
        attend(r0, bq, ckm_scr[0:1, r0:r0 + bq])
    if tail:
        attend(t_main, tail, ckt_scr[0:1, :])


def fox_prompt(q, k, v, cum):
    B, T, D = q.shape
    H = D // HEAD
    bq = 2 * LANES
    t_main = (T // bq) * bq
    tail = T - t_main
    assert tail % BF16_SUBLANES == 0
    head = pl.BlockSpec((None, T, HEAD), lambda b, h: (b, 0, h))
    vmem = 6 * T * HEAD * 4 + 2 * T * LANES * 4 + 2 * T * HEAD * 2 + 2 * T * HEAD * 2 + T * HEAD * 4 \
        + 8 * T * 4 + 6 * bq * t_main * 4
    return pl.pallas_call(
        functools.partial(_fox_kernel, T=T, H=H, bq=bq),
        out_shape=jax.ShapeDtypeStruct((B, T, D), BF16),
        grid=(B, H),
        in_specs=[head, head, head, pl.BlockSpec((None, T, H), lambda b, h: (b, 0, 0))],
        out_specs=head,
        scratch_shapes=[pltpu.VMEM((T, HEAD), BF16), pltpu.VMEM((T, HEAD), BF16),
                        pltpu.VMEM((T, HEAD), F32), pltpu.VMEM((8, t_main), F32),
                        pltpu.VMEM((8, max(tail, 8)), F32)],
        compiler_params=_params(("parallel", "parallel"), vmem),
        name="fox_prompt",
    )(q, k, v, cum)


def _page_suffix_kernel(lf_ref, within_ref, total_ref, *, pages):
    upper = (lax.broadcasted_iota(jnp.int32, (PAGE, PAGE), 0) <
             lax.broadcasted_iota(jnp.int32, (PAGE, PAGE), 1)).astype(BF16)
    ones = jnp.ones((PAGE, PAGE), BF16)
    for p in range(pages):
        hi, mid, lo = _split3(lf_ref[p])
        within_ref[p] = (_dot(upper, hi) + _dot(upper, mid)) + _dot(upper, lo)
        total_ref[p] = (_dot(ones, hi) + _dot(ones, mid)) + _dot(ones, lo)


def fox_page_suffix(cache_logf, layer):
    _, n_pool, _, H = cache_logf.shape
    pages = _divisor(n_pool, 16, 1)
    spec = pl.BlockSpec((pages, PAGE, H), lambda i: (i, 0, 0))
    return pl.pallas_call(
        functools.partial(_page_suffix_kernel, pages=pages),
        out_shape=[jax.ShapeDtypeStruct((n_pool, PAGE, H), F32)] * 2,
        grid=(n_pool // pages,),
        in_specs=[pl.BlockSpec((None, pages, PAGE, H), lambda i: (layer, i, 0, 0))],
        out_specs=[spec, spec],
        compiler_params=_params(("parallel",), 6 * pages * PAGE * LANES * 4),
        name="fox_page_suffix",
    )(cache_logf)


def _logsig_kernel(x_ref, b_ref, o_ref):
    o_ref[...] = _log_sigmoid(x_ref[...] + b_ref[...])


def fox_logf_step(fl, b_f):
    return pl.pallas_call(
        _logsig_kernel, out_shape=jax.ShapeDtypeStruct(fl.shape, F32), name="fox_logf_step",
    )(fl, b_f.reshape(1, -1).astype(F32))


def _fox_decode_kernel(pt_ref, *refs, n_pages, G, H):
    k_refs = refs[0:G]
    v_refs = refs[G:2 * G]
    w_refs = refs[2 * G:3 * G]
    t_refs = refs[3 * G:4 * G]
    q_ref, kn_ref, vn_ref, lfn_ref, lfc_ref, o_ref, m_scr, l_scr, acc_scr, carry_scr = refs[4 * G:]
    g = pl.program_id(1)
    scale = HEAD ** -0.5
    W = PAGE * H
    q = q_ref[...]
    qb = q.astype(BF16)
    lane = lax.broadcasted_iota(jnp.int32, (H, W), 1)
    lane_head = jnp.bitwise_and(lane, H - 1) if H & (H - 1) == 0 else lane % H
    own = lane_head == lax.broadcasted_iota(jnp.int32, (H, W), 0)

    @pl.when(g == 0)
    def _():
        m_scr[...] = jnp.full_like(m_scr, -jnp.inf)
        l_scr[...] = jnp.zeros_like(l_scr)
        acc_scr[...] = jnp.zeros_like(acc_scr)
        carry_scr[...] = jnp.zeros_like(carry_scr)

    def online_update(s, pv):
        m_old = m_scr[...]
        m_new = jnp.maximum(m_old, jnp.max(s, axis=-1, keepdims=True))
        alpha = jnp.exp(m_old - m_new)
        p = jnp.exp(s - m_new)
        l_scr[...] = l_scr[...] * alpha + jnp.sum(p, axis=-1, keepdims=True)
        acc_scr[...] = acc_scr[...] * alpha + pv(p)
        m_scr[...] = m_new

    for i in range(G):
        kb = k_refs[i][...].reshape(W, HEAD).astype(BF16)
        vb = v_refs[i][...].reshape(W, HEAD).astype(BF16)
        bias = lfn_ref[...] + (w_refs[i][...] + carry_scr[...])
        carry_scr[...] = carry_scr[...] + t_refs[i][...]
        s = jnp.where(own, _dot_nt(qb, kb) * scale + bias, -jnp.inf)
        online_update(s, lambda p: _dot(p.astype(BF16), vb))

    @pl.when(g == pl.num_programs(1) - 1)
    def _():
        cum_new = lfc_ref[...]
        s_new = jnp.sum(q * kn_ref[...], axis=-1, keepdims=True) * scale + (cum_new - cum_new)
        online_update(s_new, lambda p: p * vn_ref[...])
        o_ref[...] = (acc_scr[...] / l_scr[...]).astype(o_ref.dtype)


def fox_decode(q, k_new, v_new, fl_new, b_f, cache_k, cache_v, cache_logf, layer, page_table, D):
    Bd = q.shape[0]
    H = D // HEAD
    n_pages = page_table.shape[1]
    G = 4 if n_pages % 4 == 0 else 1
    W = PAGE * H
    lf_new = fox_logf_step(fl_new, b_f)
    within, total = fox_page_suffix(cache_logf, layer)
    within = within.reshape(-1, 1, W)
    total = total.reshape(-1, 1, W)

    def page_spec(block):
        def make(i):
            def index(b, g, pt):
                return block[0](pt[b, n_pages - 1 - (g * G + i)])
            return pl.BlockSpec(block[1], index)
        return [make(i) for i in range(G)]

    kv = ((lambda p: (layer, p, 0, 0, 0)), (None, None, PAGE, H, HEAD))
    flat = ((lambda p: (p, 0, 0)), (None, 1, W))
    per_seq = lambda shape: pl.BlockSpec((None,) + shape, lambda b, g, pt: (b, 0, 0))
    vmem = 2 * 2 * G * PAGE * D * 4 + 2 * G * PAGE * D * 2 + 4 * G * 8 * W * 4 + 10 * H * W * 4
    grid_spec = pltpu.PrefetchScalarGridSpec(
        num_scalar_prefetch=1,
        grid=(Bd, n_pages // G),
        in_specs=(page_spec(kv) + page_spec(kv) + page_spec(flat) + page_spec(flat)
                  + [per_seq((H, HEAD))] * 3 + [per_seq((1, W)), per_seq((H, 1))]),
        out_specs=per_seq((H, HEAD)),
        scratch_shapes=[pltpu.VMEM((H, 1), F32), pltpu.VMEM((H, 1), F32), pltpu.VMEM((H, HEAD), F32),
                        pltpu.VMEM((1, W), F32)],
    )
    heads = lambda x: x.reshape(Bd, H, HEAD)
    o = pl.pallas_call(
        functools.partial(_fox_decode_kernel, n_pages=n_pages, G=G, H=H),
        out_shape=jax.ShapeDtypeStruct((Bd, H, HEAD), BF16),
        grid_spec=grid_spec,
        compiler_params=_params(("parallel", "arbitrary"), vmem),
        name="fox_decode",
    )(page_table, *([cache_k] * G), *([cache_v] * G), *([within] * G), *([total] * G),
      heads(q), heads(k_new), heads(v_new), jnp.tile(lf_new, (1, PAGE)).reshape(Bd, 1, W),
      lf_new.reshape(Bd, H, 1))
    return o.reshape(Bd, D), lf_new


def _ffn(h, norm_w, w_gu, w_down, layer):
    a = rmsnorm(h, norm_w, BF16)
    return matmul(matmul_swiglu(a, w_gu, layer), w_down, layer, residual=h)


def kernel(x_prompt, x_sample, cache_k, cache_v, cache_logf, page_table, state_rec, state_conv,
           meta_tokens, norm_mix, norm_ffn, norm_final, w_in_a, conv_w_a, a_log_a, dt_bias_a,
           onorm_a, w_out_a, w_in_b, b_f, w_out_b, w_gu, w_down):
    B, S, D = x_prompt.shape
    Bd, Q, _ = x_sample.shape
    assert Q == 1, "the sample group is a single-token step"
    H = D // HEAD
    T = S + N_META_TOK
    M = B * T
    meta = jnp.broadcast_to(meta_tokens.astype(x_prompt.dtype)[None], (B, N_META_TOK, D))
    hp = jnp.concatenate([meta, x_prompt], axis=1).reshape(M, D)
    hs = x_sample.reshape(Bd, D)
    w_down_b = w_down.astype(BF16)

    w_ab = w_in_a[:, :, 4 * D:]
    ap = rmsnorm(hp, norm_mix[0], BF16)
    a_s = rmsnorm(hs, norm_mix[0], BF16)
    proj_p = matmul(ap, w_in_a, cols=(0, 4 * D)).reshape(B, T, 4 * D)
    ab_p = matmul(ap, w_ab).reshape(B, T, 2 * H)
    proj_s = matmul(a_s, w_in_a, cols=(0, 4 * D))
    ab_s = matmul(a_s, w_ab)

    qkv_c = gdn_conv_prompt(proj_p, conv_w_a[0], D)
    on_p, rec_p = gdn_prompt(qkv_c, proj_p, ab_p, a_log_a[0], dt_bias_a[0], onorm_a[0], D)
    conv_p = proj_p[:, T - (CONV_TAPS - 1):, :3 * D]
    on_s, rec_s = gdn_sample(proj_s, ab_s, state_rec[0], state_conv[0], conv_w_a[0], a_log_a[0],
                             dt_bias_a[0], onorm_a[0], D)
    conv_s = jnp.concatenate([state_conv[0][:, 1:], proj_s[:, None, :3 * D].astype(state_conv.dtype)], axis=1)

    hp = matmul(on_p.reshape(M, D), w_out_a, residual=hp)
    hs = matmul(on_s.reshape(Bd, D), w_out_a, residual=hs)
    hp = _ffn(hp, norm_ffn[0], w_gu, w_down_b, 0)
    hs = _ffn(hs, norm_ffn[0], w_gu, w_down_b, 0)

    w_f = w_in_b[:, :, 3 * D:]
    ap = rmsnorm(hp, norm_mix[1], BF16)
    a_s = rmsnorm(hs, norm_mix[1], BF16)
    q_p, k_p, v_p = (matmul(ap, w_in_b, cols=(part * D, D)).reshape(B, T, D) for part in range(3))
    fl_p = matmul(ap, w_f).reshape(B, T, H)
    qkv_s = matmul(a_s, w_in_b, cols=(0, 3 * D))
    fl_s = matmul(a_s, w_f)

    lf_p, cum_p = fox_logf(fl_p, b_f[0])
    o_p = fox_prompt(q_p, k_p, v_p, cum_p)
    o_s, lf_s = fox_decode(qkv_s[:, :D], qkv_s[:, D:2 * D], qkv_s[:, 2 * D:], fl_s, b_f[0],
                           cache_k, cache_v, cache_logf, 0, page_table, D)

    hp = matmul(o_p.reshape(M, D), w_out_b, residual=hp)
    hs = matmul(o_s, w_out_b, residual=hs)
    hp = _ffn(hp, norm_ffn[1], w_gu, w_down_b, 1)
    hs = _ffn(hs, norm_ffn[1], w_gu, w_down_b, 1)

    y_prompt = rmsnorm(hp, norm_final, F32).reshape(B, T, D)[:, N_META_TOK:]
    y_sample = rmsnorm(hs, norm_final, F32).reshape(Bd, 1, D)

    k_prompt = k_p.reshape(1, B, T, H, HEAD)
    v_prompt = v_p.reshape(1, B, T, H, HEAD)
    logf_prompt = lf_p.astype(cache_logf.dtype)[None]
    k_sample = qkv_s[:, D:2 * D].reshape(1, Bd, 1, H, HEAD)
    v_sample = qkv_s[:, 2 * D:].reshape(1, Bd, 1, H, HEAD)
    logf_sample = lf_s.astype(cache_logf.dtype).reshape(1, Bd, 1, H)
    return (y_prompt, y_sample, k_prompt, v_prompt, logf_prompt, k_sample, v_sample, logf_sample,
            rec_p[None], conv_p[None], rec_s.astype(state_rec.dtype)[None], conv_s[None])
```

```python
import functools
import math

import jax
import jax.numpy as jnp
from jax import lax
from jax.experimental import pallas as pl
from jax.experimental.pallas import tpu as pltpu

F32 = jnp.float32
BF16 = jnp.bfloat16
HI = lax.Precision.HIGHEST

HEAD = 128
N_META_TOK = 16
CONV_TAPS = 4
GDN_CHUNK = 64
PAGE = 128
EPS = 1e-6

VMEM_BYTES_V7X = 64 << 20
VMEM_LIMIT_MAX = VMEM_BYTES_V7X - (8 << 20)
BF16_SUBLANES = 16
LANES = 128


def _dot(a, b, precision=None):
    return jnp.dot(a, b, preferred_element_type=F32, precision=precision)


def _dot_nt(a, b, precision=None):
    return lax.dot_general(a, b, (((1,), (1,)), ((), ())), preferred_element_type=F32,
                           precision=precision)


def _dot_tn(a, b, precision=None):
    return lax.dot_general(a, b, (((0,), (0,)), ((), ())), preferred_element_type=F32,
                           precision=precision)


def _split2(x):
    hi = x.astype(BF16)
    return hi, (x - hi.astype(F32)).astype(BF16)


def _split3(x):
    hi = x.astype(BF16)
    r = x - hi.astype(F32)
    mid = r.astype(BF16)
    return hi, mid, (r - mid.astype(F32)).astype(BF16)


def _dot_sel(sel, x, dot=_dot):
    hi, mid, lo = _split3(x)
    return (dot(sel, hi) + dot(sel, mid)) + dot(sel, lo)


def _dot_sel_r(x, sel, dot=_dot):
    hi, mid, lo = _split3(x)
    return (dot(hi, sel) + dot(mid, sel)) + dot(lo, sel)


def _dot_x3(a, b):
    ah, al = _split2(a)
    bh, bl = _split2(b)
    return _dot(ah, bh) + (_dot(ah, bl) + _dot(al, bh))


def _sigmoid(x):
    return 1.0 / (1.0 + jnp.exp(-x))


def _silu(x):
    return x * _sigmoid(x)


def _softplus(x):
    return jnp.maximum(x, 0.0) + jnp.log1p(jnp.exp(-jnp.abs(x)))


def _log_sigmoid(x):
    return -_softplus(-x)


def _divisor(n, cap, align):
    best = None
    for d in range(align, min(n, cap) + 1, align):
        if n % d == 0:
            best = d
    return n if best is None else best


def _params(semantics, vmem_bytes):
    limit = int(min(max(vmem_bytes * 5 // 4 + (4 << 20), 32 << 20), VMEM_LIMIT_MAX))
    return pltpu.CompilerParams(dimension_semantics=semantics, vmem_limit_bytes=limit)


def _rmsnorm_kernel(x_ref, w_ref, o_ref):
    x = x_ref[...]
    y = x * lax.rsqrt(jnp.mean(x * x, axis=-1, keepdims=True) + EPS)
    o_ref[...] = (y * w_ref[...]).astype(o_ref.dtype)


def rmsnorm(x, w, out_dtype):
    M, D = x.shape
    bm = _divisor(M, max(BF16_SUBLANES, (3 << 20) // (4 * D)), BF16_SUBLANES)
    vmem = 2 * bm * D * (4 + jnp.dtype(out_dtype).itemsize)
    return pl.pallas_call(
        _rmsnorm_kernel,
        out_shape=jax.ShapeDtypeStruct((M, D), out_dtype),
        grid=(M // bm,),
        in_specs=[pl.BlockSpec((bm, D), lambda i: (i, 0)),
                  pl.BlockSpec((1, D), lambda i: (0, 0))],
        out_specs=pl.BlockSpec((bm, D), lambda i: (i, 0)),
        compiler_params=_params(("parallel",), vmem),
        name="rmsnorm",
    )(x, w.reshape(1, D).astype(F32))


SAMPLE_ROWS = BF16_SUBLANES


def _mm_body(n_weights, has_residual, has_sample):
    def body(*refs):
        refs = list(refs)
        x_ref = refs.pop(0)
        w_refs = [refs.pop(0) for _ in range(n_weights)]
        r_ref = refs.pop(0) if has_residual else None
        xs_ref = refs.pop(0) if has_sample else None
        rs_ref = refs.pop(0) if has_sample and has_residual else None
        o_ref = refs.pop(0)
        os_ref = refs.pop(0) if has_sample else None

        def compute(x_r, res_r, out_r):
            ys = [_dot(x_r[...], w_r[...]) for w_r in w_refs]
            y = ys[0] if n_weights == 1 else _silu(ys[0]) * ys[1]
            out_r[...] = (y if res_r is None else res_r[...] + y).astype(out_r.dtype)

        compute(x_ref, r_ref, o_ref)
        if has_sample:
            @pl.when(pl.program_id(0) == 0)
            def _():
                compute(xs_ref, rs_ref, os_ref)

            @pl.when(pl.program_id(0) != 0)
            def _():
                os_ref[...] = jnp.zeros_like(os_ref)

    return body


MM_VMEM_BUDGET = 48 << 20


def _mm_vmem(bm, bn, K, n_weights, out_bytes, residual):
    return (2 * bm * K * 2 + 2 * n_weights * K * bn * 2 + 2 * bm * bn * out_bytes
            + (2 + n_weights) * bm * bn * 4 + (2 * bm * bn * 4 if residual else 0))


def _mm_tiles(M, K, N, n_weights=1, out_bytes=4, residual=False):
    bm = _divisor(M, max(BF16_SUBLANES, (16 << 20) // (2 * K)), BF16_SUBLANES)
    if N % LANES:
        return bm, N
    bn = LANES
    for cand in (4 * LANES, 2 * LANES):
        if N % cand == 0 and _mm_vmem(bm, cand, K, n_weights, out_bytes, residual) <= MM_VMEM_BUDGET:
            bn = cand
            break
    return bm, bn


def _matmul(x, w, layer, col_blocks, N, residual, out_dtype, sample, name):
    M, K = x.shape
    n_weights = len(col_blocks(LANES))
    out_bytes = jnp.dtype(out_dtype).itemsize
    bm, bn = _mm_tiles(M, K, N, n_weights, out_bytes, residual is not None)
    tile = pl.BlockSpec((bm, bn), lambda i, j: (i, j))
    in_specs = [pl.BlockSpec((bm, K), lambda i, j: (i, 0))]
    in_specs += [pl.BlockSpec((None, K, bn), functools.partial(lambda i, j, j0: (layer, 0, j + j0), j0=j0))
                 for j0 in col_blocks(bn)]
    args = [x] + [w] * n_weights
    out_shape = [jax.ShapeDtypeStruct((M, N), out_dtype)]
    out_specs = [tile]
    if residual is not None:
        in_specs.append(tile)
        args.append(residual)
    if sample is not None:
        xs, rs = sample
        in_specs.append(pl.BlockSpec((SAMPLE_ROWS, K), lambda i, j: (0, 0)))
        args.append(xs)
        if residual is not None:
            in_specs.append(pl.BlockSpec((SAMPLE_ROWS, bn), lambda i, j: (0, j)))
            args.append(rs)
        out_shape.append(jax.ShapeDtypeStruct((M // bm * SAMPLE_ROWS, N), out_dtype))
        out_specs.append(pl.BlockSpec((SAMPLE_ROWS, bn), lambda i, j: (i, j)))
    outs = pl.pallas_call(
        _mm_body(n_weights, residual is not None, sample is not None),
        out_shape=out_shape,
        grid=(M // bm, N // bn),
        in_specs=in_specs,
        out_specs=out_specs,
        compiler_params=_params(("parallel", "parallel"),
                                _mm_vmem(bm, bn, K, n_weights, out_bytes, residual is not None)),
        name=name,
    )(*args)
    return outs[0] if sample is None else (outs[0], outs[1][:SAMPLE_ROWS])


def matmul(x, w, layer=0, residual=None, out_dtype=F32, cols=None, sample=None):
    col0, N = (0, w.shape[2]) if cols is None else cols
    assert col0 + N <= w.shape[2]

    def col_blocks(bn):
        assert col0 % bn == 0 or N % LANES
        return [col0 // bn]

    return _matmul(x, w, layer, col_blocks, N, residual, out_dtype, sample, "matmul")


def matmul_swiglu(x, w_gu, layer, sample=None):
    F = w_gu.shape[2] // 2
    return _matmul(x, w_gu, layer, lambda bn: [0, F // bn], F, None, BF16, sample, "matmul_swiglu")


def _conv_kernel(x_ref, w_ref, o_ref, *, T, rows, heads_per_block, blocks_per_part):
    kind = pl.program_id(1) // blocks_per_part
    w = w_ref[...]
    scale = jnp.where(kind == 0, HEAD ** -0.5, 1.0).astype(F32)
    halo = 8

    def conv_silu(i):
        r0 = pl.multiple_of(i * rows, rows)
        cur = x_ref[pl.ds(r0, rows), :]
        prev = x_ref[pl.ds(pl.multiple_of(jnp.maximum(r0 - halo, 0), halo), halo), :]
        xx = jnp.concatenate([jnp.where(i > 0, prev, 0.0), cur], axis=0)
        acc = None
        for j in range(CONV_TAPS):
            s = CONV_TAPS - 1 - j
            xs = cur if s == 0 else pltpu.roll(xx, s, 0)[halo:, :]
            term = xs * w[j:j + 1, :]
            acc = term if acc is None else acc + term
        return r0, _silu(acc)

    def qk_chunk(i, carry):
        r0, y = conv_silu(i)
        for h in range(heads_per_block):
            seg = y[:, h * HEAD:(h + 1) * HEAD]
            r = lax.rsqrt(jnp.sum(seg * seg, axis=-1, keepdims=True) + EPS)
            o_ref[pl.ds(r0, rows), h * HEAD:(h + 1) * HEAD] = seg * r * scale
        return carry

    def v_chunk(i, carry):
        r0, y = conv_silu(i)
        o_ref[pl.ds(r0, rows), :] = y
        return carry

    n = T // rows
    unroll = 2

    @pl.when(kind < 2)
    def _():
        lax.fori_loop(0, n, qk_chunk, 0, unroll=unroll)

    @pl.when(kind == 2)
    def _():
        lax.fori_loop(0, n, v_chunk, 0, unroll=unroll)


def gdn_conv_prompt(proj, conv_w, D):
    B, T, _ = proj.shape
    cb = _divisor(D, 512, HEAD)
    rows = _divisor(T, 64, 8)
    vmem = 4 * T * cb * 4
    return pl.pallas_call(
        functools.partial(_conv_kernel, T=T, rows=rows, heads_per_block=cb // HEAD,
                          blocks_per_part=D // cb),
        out_shape=jax.ShapeDtypeStruct((B, T, 3 * D), F32),
        grid=(B, 3 * D // cb),
        in_specs=[pl.BlockSpec((None, T, cb), lambda b, j: (b, 0, j)),
                  pl.BlockSpec((CONV_TAPS, cb), lambda b, j: (0, j))],
        out_specs=pl.BlockSpec((None, T, cb), lambda b, j: (b, 0, j)),
        compiler_params=_params(("parallel", "parallel"), vmem),
        name="gdn_conv",
    )(proj, conv_w)


def _each_dot_sel(sel, xs):
    sel3 = jnp.concatenate([sel, sel, sel], axis=1).astype(BF16)
    stacked = [jnp.concatenate(_split3(x), axis=0) for x in xs]
    return [_dot(sel3, s) for s in stacked]


def _each_dot_x3(as_, bs):
    lhs, rhs = [], []
    for a, b in zip(as_, bs):
        ah = a.astype(BF16).astype(F32)
        lhs.append(jnp.concatenate([ah, ah, a - ah], axis=1).astype(BF16))
        bh, bl = _split2(b)
        rhs.append(jnp.concatenate([bh, bl, bh], axis=0))
    return [_dot(l, r) for l, r in zip(lhs, rhs)]


def _each_neumann_inverse(bms, L):
    eye = (lax.broadcasted_iota(jnp.int32, (L, L), 0) ==
           lax.broadcasted_iota(jnp.int32, (L, L), 1)).astype(F32)
    ps = [-bm for bm in bms]
    ts = [eye + p for p in ps]
    for _ in range(int(math.log2(L)) - 1):
        ps = _each_dot_x3(ps, ps)
        ts = [t + d for t, d in zip(ts, _each_dot_x3(ts, ps))]
    return ts


def _gdn_kernel(alog_ref, dtb_ref, q_ref, k_ref, v_ref, z_ref, ab_ref, onorm_ref,
                o_ref, s_out_ref,
                g_scr, b_scr, w_scr, u_scr, eg_scr, a_scr, c_scr, qk_scr, qkm_scr, s_scr,
                *, T, H, hb, n_meta, chunk, prep_unroll):
    n_chunks = (T - n_meta) // chunk
    heads = [(hh, pl.program_id(1) * hb + hh, slice(hh * HEAD, (hh + 1) * HEAD)) for hh in range(hb)]

    ab_parts = _split3(ab_ref[...])
    sel_row = lax.broadcasted_iota(jnp.int32, (2 * H, 2 * HEAD), 0)
    sel_col = lax.broadcasted_iota(jnp.int32, (2 * H, 2 * HEAD), 1)
    for _, h, lanes in heads:
        sel = (sel_row == jnp.where(sel_col < HEAD, h, h + H)).astype(BF16)
        ab_col = (_dot(ab_parts[0], sel) + _dot(ab_parts[1], sel)) + _dot(ab_parts[2], sel)
        decay_rate = jnp.exp(jnp.full((1, HEAD), alog_ref[h], F32))
        g_scr[:, lanes] = -decay_rate * _softplus(ab_col[:, :HEAD] + dtb_ref[h])
        b_scr[:, lanes] = _sigmoid(ab_col[:, HEAD:])

    def prep(offs, L, qk_ref):
        ri = lax.broadcasted_iota(jnp.int32, (L, L), 0)
        ci = lax.broadcasted_iota(jnp.int32, (L, L), 1)
        tri = ri >= ci
        strict = ri > ci
        eye = ri == ci
        jobs = [(pl.ds(off, L), pl.ds(qk_off, L), hh, lanes) for off, qk_off, _ in offs for hh, _, lanes in heads]
        slots = [slot for _, _, slot in offs for _ in heads]
        ks = [k_ref[rows, lanes] for rows, _, _, lanes in jobs]
        betas = [b_scr[rows, lanes] for rows, _, _, lanes in jobs]
        gcs = _each_dot_sel(tri.astype(F32), [g_scr[rows, lanes] for rows, _, _, lanes in jobs])
        gc_is = [gc[:, :L] for gc in gcs]
        gc_js = _each_dot_sel(jnp.ones((L, L), F32), [jnp.where(eye, g, 0.0) for g in gc_is])
        decays = [jnp.where(tri, jnp.exp(jnp.where(tri, gi - gj, 0.0)), 0.0) for gi, gj in zip(gc_is, gc_js)]
        kbs = [k.astype(BF16) for k in ks]
        kks = [_dot_nt(kb, kb) for kb in kbs]
        bms = [jnp.where(strict, kk * d, 0.0) * beta[:, :L] for kk, d, beta in zip(kks, decays, betas)]
        ts = _each_neumann_inverse(bms, L)
        egs = [jnp.exp(gc) for gc in gcs]
        wus = _each_dot_x3(ts, [jnp.concatenate([beta * eg * k, beta * v_ref[rows, lanes]], axis=1)
                                for beta, eg, k, (rows, _, _, lanes) in zip(betas, egs, ks, jobs)])
        ws = [wu[:, :HEAD] for wu in wus]
        us = [wu[:, HEAD:] for wu in wus]
        qks = [_dot_nt(q_ref[rows, lanes].astype(BF16), kb) * d
               for (rows, _, _, lanes), kb, d in zip(jobs, kbs, decays)]
        kts = [(k * jnp.exp(gc[L - 1:L, :] - gc)).astype(BF16) for k, gc in zip(ks, gcs)]
        acs = [_dot_tn(kt, wu.astype(BF16)) for kt, wu in zip(kts, wus)]
        for (rows, qk_rows, hh, lanes), slot, w, u, eg, ac, qk in zip(jobs, slots, ws, us, egs, acs, qks):
            w_scr[rows, lanes] = w
            u_scr[rows, lanes] = u
            eg_scr[rows, lanes] = eg
            a_scr[hh, slot] = ac[:, :HEAD].astype(BF16)
            c_scr[hh, slot] = ac[:, HEAD:]
            qk_ref[hh, qk_rows, :] = qk

    def advance(off, L, slot):
        rows = pl.ds(off, L)
        ss = [s_scr[hh] for hh, _, _ in heads]
        lhs = [jnp.concatenate([a_scr[hh, slot], w_scr[rows, lanes].astype(BF16),
                                q_ref[rows, lanes].astype(BF16)], axis=0) for hh, _, lanes in heads]
        prods = [_dot(l, s.astype(BF16)) for l, s in zip(lhs, ss)]
        for (hh, _, lanes), s, p in zip(heads, ss, prods):
            eg_last = eg_scr[pl.ds(off + L - 8, 8), lanes][7:8, :]
            s_scr[hh] = (eg_last * s - p[:HEAD]) + c_scr[hh, slot]
        return tuple(x for (_, _, lanes), p in zip(heads, prods)
                     for x in (u_scr[rows, lanes] - p[HEAD:HEAD + L], p[HEAD + L:]))

    def emit(off, L, qk_ref, qk_off, carried):
        rows = pl.ds(off, L)
        for i, (hh, _, lanes) in enumerate(heads):
            u, qs = carried[2 * i], carried[2 * i + 1]
            o = eg_scr[rows, lanes] * qs + _dot(qk_ref[hh, pl.ds(qk_off, L), :].astype(BF16), u.astype(BF16))
            on = o * lax.rsqrt(jnp.mean(o * o, axis=-1, keepdims=True) + EPS) * onorm_ref[...]
            o_ref[rows, lanes] = (on * _silu(z_ref[rows, lanes])).astype(o_ref.dtype)

    def chunk_off(c):
        off = n_meta + c * chunk
        return off if isinstance(c, int) else pl.multiple_of(off, BF16_SUBLANES)

    def chunk_qk_off(c):
        return c * chunk if isinstance(c, int) else pl.multiple_of(c * chunk, chunk)

    prep([(0, 0, 0)], n_meta, qkm_scr)

    def prep_body(i, carry):
        cs = [i * prep_unroll + j for j in range(prep_unroll)]
        prep([(chunk_off(c), chunk_qk_off(c), c + 1) for c in cs], chunk, qk_scr)
        return carry

    lax.fori_loop(0, n_chunks // prep_unroll, prep_body, 0)

    s_scr[...] = jnp.zeros_like(s_scr)
    emit(0, n_meta, qkm_scr, 0, advance(0, n_meta, 0))

    def step(c, carried):
        new = advance(chunk_off(c), chunk, c + 1)
        emit(chunk_off(c - 1), chunk, qk_scr, chunk_qk_off(c - 1), carried)
        return new

    last = lax.fori_loop(1, n_chunks, step, advance(chunk_off(0), chunk, 1))
    emit(chunk_off(n_chunks - 1), chunk, qk_scr, chunk_qk_off(n_chunks - 1), last)
    s_out_ref[...] = s_scr[...]


def gdn_prompt(qkv_c, proj, ab, a_log, dt_bias, onorm, D):
    B, T, _ = qkv_c.shape
    H = D // HEAD
    assert (T - N_META_TOK) % GDN_CHUNK == 0
    n_chunks = (T - N_META_TOK) // GDN_CHUNK
    hb = 2 if H % 2 == 0 else 1
    prep_unroll = _divisor(n_chunks, 8, 1)
    W = hb * HEAD
    col = lambda part: pl.BlockSpec((None, T, W), lambda b, g: (b, 0, part * (H // hb) + g))
    smem = pl.BlockSpec(memory_space=pltpu.SMEM)
    vmem = ((8 + 5) * T * W * 4 + 2 * T * LANES * 4 + 2 * T * W * 2 + hb * n_chunks * GDN_CHUNK * LANES * 4
            + hb * (n_chunks + 1) * HEAD * HEAD * 6)
    return pl.pallas_call(
        functools.partial(_gdn_kernel, T=T, H=H, hb=hb, n_meta=N_META_TOK, chunk=GDN_CHUNK,
                          prep_unroll=prep_unroll),
        out_shape=[jax.ShapeDtypeStruct((B, T, D), BF16),
                   jax.ShapeDtypeStruct((B, H, HEAD, HEAD), F32)],
        grid=(B, H // hb),
        in_specs=[smem, smem, col(0), col(1), col(2), col(3),
                  pl.BlockSpec((None, T, 2 * H), lambda b, g: (b, 0, 0)),
                  pl.BlockSpec((1, HEAD), lambda b, g: (0, 0))],
        out_specs=[pl.BlockSpec((None, T, W), lambda b, g: (b, 0, g)),
                   pl.BlockSpec((None, hb, HEAD, HEAD), lambda b, g: (b, g, 0, 0))],
        scratch_shapes=[pltpu.VMEM((T, W), F32)] * 5 + [
            pltpu.VMEM((hb, n_chunks + 1, HEAD, HEAD), BF16),
            pltpu.VMEM((hb, n_chunks + 1, HEAD, HEAD), F32),
            pltpu.VMEM((hb, n_chunks * GDN_CHUNK, GDN_CHUNK), F32),
            pltpu.VMEM((hb, N_META_TOK, N_META_TOK), F32),
            pltpu.VMEM((hb, HEAD, HEAD), F32)],
        compiler_params=_params(("parallel", "parallel"), vmem),
        name="gdn_chunk",
    )(a_log.astype(F32), dt_bias.astype(F32), qkv_c, qkv_c, qkv_c, proj, ab, onorm.reshape(1, HEAD).astype(F32))


def _gdn_step_kernel(u_ref, cs_ref, w_ref, ab_ref, alog_ref, dtb_ref, onorm_ref, s_ref,
                     o_ref, s_out_ref, q_scr, k_scr, v_scr, g_scr, b_scr, o_scr, *, H):
    u_all = u_ref[...]
    u = u_all[:3 * H]
    xc = (cs_ref[0] * w_ref[0] + cs_ref[1] * w_ref[1] + cs_ref[2] * w_ref[2]) + u * w_ref[3]
    xc = _silu(xc)
    q = xc[:H]
    k = xc[H:2 * H]
    q_scr[...] = q * lax.rsqrt(jnp.sum(q * q, axis=-1, keepdims=True) + EPS) * (HEAD ** -0.5)
    k_scr[...] = k * lax.rsqrt(jnp.sum(k * k, axis=-1, keepdims=True) + EPS)
    v_scr[...] = xc[2 * H:]
    g_scr[...] = jnp.broadcast_to(-jnp.exp(alog_ref[...]) * _softplus(ab_ref[0] + dtb_ref[...]), (H, HEAD))
    b_scr[...] = jnp.broadcast_to(_sigmoid(ab_ref[1]), (H, HEAD))
    eye = (lax.broadcasted_iota(jnp.int32, (HEAD, HEAD), 0) ==
           lax.broadcasted_iota(jnp.int32, (HEAD, HEAD), 1)).astype(F32)

    group = 8 if H % 8 == 0 else 1

    def heads_step(gi, carry):
        hs = [gi * group + j for j in range(group)]
        rows = [pl.ds(h, 1) for h in hs]
        k_rows = [k_scr[r, :] for r in rows]
        ss = [s_ref[h] * jnp.exp(g_scr[r, :]) for h, r in zip(hs, rows)]
        kss = [_dot(k_row, s, HI) for k_row, s in zip(k_rows, ss)]
        upds = [b_scr[r, :] * (v_scr[r, :] - ks) for r, ks in zip(rows, kss)]
        k_cols = [_dot_nt(eye, jnp.broadcast_to(k_row, (HEAD, HEAD)), HI) for k_row in k_rows]
        ss = [s + k_col * upd for s, k_col, upd in zip(ss, k_cols, upds)]
        outs = [_dot(q_scr[r, :], s, HI) for r, s in zip(rows, ss)]
        for h, r, s, o in zip(hs, rows, ss, outs):
            s_out_ref[h] = s
            o_scr[r, :] = o
        return carry

    lax.fori_loop(0, H // group, heads_step, 0)
    o = o_scr[...]
    on = o * lax.rsqrt(jnp.mean(o * o, axis=-1, keepdims=True) + EPS) * onorm_ref[...]
    o_ref[...] = (on * _silu(u_all[3 * H:])).astype(o_ref.dtype)


def gdn_sample(proj, ab, state_rec, state_conv, conv_w, a_log, dt_bias, onorm, D):
    Bd = proj.shape[0]
    H = D // HEAD
    col = lambda x: x.astype(F32).reshape(H, 1)
    vmem = 2 * (4 * H * HEAD * 4 + 3 * 3 * H * HEAD * 4 + 2 * H * HEAD * HEAD * 4) + 4 * 3 * H * HEAD * 4
    return pl.pallas_call(
        functools.partial(_gdn_step_kernel, H=H),
        out_shape=[jax.ShapeDtypeStruct((Bd, H, HEAD), BF16),
                   jax.ShapeDtypeStruct((Bd, H, HEAD, HEAD), F32)],
        grid=(Bd,),
        in_specs=[pl.BlockSpec((None, 4 * H, HEAD), lambda b: (b, 0, 0)),
                  pl.BlockSpec((None, CONV_TAPS - 1, 3 * H, HEAD), lambda b: (b, 0, 0, 0)),
                  pl.BlockSpec((CONV_TAPS, 3 * H, HEAD), lambda b: (0, 0, 0)),
                  pl.BlockSpec((None, 2, H, 1), lambda b: (b, 0, 0, 0)),
                  pl.BlockSpec((H, 1), lambda b: (0, 0)),
                  pl.BlockSpec((H, 1), lambda b: (0, 0)),
                  pl.BlockSpec((1, HEAD), lambda b: (0, 0)),
                  pl.BlockSpec((None, H, HEAD, HEAD), lambda b: (b, 0, 0, 0))],
        out_specs=[pl.BlockSpec((None, H, HEAD), lambda b: (b, 0, 0)),
                   pl.BlockSpec((None, H, HEAD, HEAD), lambda b: (b, 0, 0, 0))],
        scratch_shapes=[pltpu.VMEM((H, HEAD), F32)] * 6,
        compiler_params=_params(("parallel",), vmem),
        name="gdn_step",
    )(proj.reshape(Bd, 4 * H, HEAD), state_conv.reshape(Bd, CONV_TAPS - 1, 3 * H, HEAD),
      conv_w.reshape(CONV_TAPS, 3 * H, HEAD), ab.reshape(Bd, 2, H, 1), col(a_log), col(dt_bias),
      onorm.reshape(1, HEAD).astype(F32), state_rec)


def _logf_kernel(fl_ref, bf_ref, lf_ref, cum_ref, *, T):
    carry = jnp.zeros((1, fl_ref.shape[-1]), F32)
    for r0 in range(0, T, LANES):
        n = min(LANES, T - r0)
        lf = _log_sigmoid(fl_ref[r0:r0 + n, :] + bf_ref[...])
        lf_ref[r0:r0 + n, :] = lf
        tri = (lax.broadcasted_iota(jnp.int32, (n, n), 0) >=
               lax.broadcasted_iota(jnp.int32, (n, n), 1)).astype(F32)
        cum = _dot(tri, lf, HI) + carry
        cum_ref[r0:r0 + n, :] = cum
        carry = cum[n - 1:n, :]


def fox_logf(fl, b_f):
    B, T, H = fl.shape
    spec = pl.BlockSpec((None, T, H), lambda b: (b, 0, 0))
    return pl.pallas_call(
        functools.partial(_logf_kernel, T=T),
        out_shape=[jax.ShapeDtypeStruct((B, T, H), F32)] * 2,
        grid=(B,),
        in_specs=[spec, pl.BlockSpec((1, H), lambda b: (0, 0))],
        out_specs=[spec, spec],
        compiler_params=_params(("parallel",), 6 * T * LANES * 4),
        name="fox_logf",
    )(fl, b_f.reshape(1, H).astype(F32))


def _fox_kernel(q_ref, k_ref, v_ref, cum_ref, o_ref, kb_scr, vb_scr, cq_scr, ckm_scr, ckt_scr,
                *, T, H, bq):
    h = pl.program_id(1)
    t_main = (T // bq) * bq
    tail = T - t_main
    scale = HEAD ** -0.5
    kb_scr[...] = k_ref[...].astype(BF16)
    vb_scr[...] = v_ref[...].astype(BF16)
    cum = cum_ref[...]
    cq_scr[...] = _dot_sel_r(cum, (lax.broadcasted_iota(jnp.int32, (H, HEAD), 0) == h).astype(BF16))
    pick = (lax.broadcasted_iota(jnp.int32, (8, H), 1) == h).astype(BF16)
    ckm_scr[...] = _dot_sel(pick, cum[:t_main], _dot_nt)
    if tail:
        ckt_scr[...] = _dot_sel(pick, cum[t_main:], _dot_nt)

    def attend(r0, nq, ck_diag):
        rows = slice(r0, r0 + nq)
        qb = q_ref[rows, :].astype(BF16)
        cq = cq_scr[rows, 0:1]
        causal = (lax.broadcasted_iota(jnp.int32, (nq, nq), 0) >=
                  lax.broadcasted_iota(jnp.int32, (nq, nq), 1))
        s_d = jnp.where(causal, _dot_nt(qb, kb_scr[rows, :]) * scale + cq - ck_diag, -jnp.inf)
        m = jnp.max(s_d, axis=-1, keepdims=True)
        if r0:
            s_o = _dot_nt(qb, kb_scr[:r0, :]) * scale + cq - ckm_scr[0:1, :r0]
            m = jnp.maximum(m, jnp.max(s_o, axis=-1, keepdims=True))
        e_d = jnp.exp(s_d - m)
        l = jnp.sum(e_d, axis=-1, keepdims=True)
        o = _dot(e_d.astype(BF16), vb_scr[rows, :])
        if r0:
            e_o = jnp.exp(s_o - m)
            l = l + jnp.sum(e_o, axis=-1, keepdims=True)
            o = o + _dot(e_o.astype(BF16), vb_scr[:r0, :])
        o_ref[rows, :] = (o / l).astype(o_ref.dtype)

    for r0 in range(0, t_main, bq):
        attend(r0, bq, ckm_scr[0:1, r0:r0 + bq])
    if tail:
        attend(t_main, tail, ckt_scr[0:1, :])


def fox_prompt(q, k, v, cum):
    B, T, D = q.shape
    H = D // HEAD
    bq = 2 * LANES
    t_main = (T // bq) * bq
    tail = T - t_main
    assert tail % BF16_SUBLANES == 0
    head = pl.BlockSpec((None, T, HEAD), lambda b, h: (b, 0, h))
    vmem = 6 * T * HEAD * 4 + 2 * T * LANES * 4 + 2 * T * HEAD * 2 + 2 * T * HEAD * 2 + T * HEAD * 4 \
        + 8 * T * 4 + 6 * bq * t_main * 4
    return pl.pallas_call(
        functools.partial(_fox_kernel, T=T, H=H, bq=bq),
        out_shape=jax.ShapeDtypeStruct((B, T, D), BF16),
        grid=(B, H),
        in_specs=[head, head, head, pl.BlockSpec((None, T, H), lambda b, h: (b, 0, 0))],
        out_specs=head,
        scratch_shapes=[pltpu.VMEM((T, HEAD), BF16), pltpu.VMEM((T, HEAD), BF16),
                        pltpu.VMEM((T, HEAD), F32), pltpu.VMEM((8, t_main), F32),
                        pltpu.VMEM((8, max(tail, 8)), F32)],
        compiler_params=_params(("parallel", "parallel"), vmem),
        name="fox_prompt",
    )(q, k, v, cum)


def _page_suffix_kernel(lf_ref, within_ref, total_ref, *, pages):
    upper = (lax.broadcasted_iota(jnp.int32, (PAGE, PAGE), 0) <
             lax.broadcasted_iota(jnp.int32, (PAGE, PAGE), 1)).astype(BF16)
    block = jnp.concatenate([upper, jnp.ones((PAGE, PAGE), BF16)], axis=0)
    sel = jnp.concatenate([block, block, block], axis=1)
    for p in range(pages):
        both = _dot(sel, jnp.concatenate(_split3(lf_ref[p]), axis=0))
        within_ref[p] = both[:PAGE]
        total_ref[p] = both[PAGE:]


def fox_page_suffix(cache_logf, layer):
    _, n_pool, _, H = cache_logf.shape
    pages = _divisor(n_pool, 16, 1)
    spec = pl.BlockSpec((pages, PAGE, H), lambda i: (i, 0, 0))
    return pl.pallas_call(
        functools.partial(_page_suffix_kernel, pages=pages),
        out_shape=[jax.ShapeDtypeStruct((n_pool, PAGE, H), F32)] * 2,
        grid=(n_pool // pages,),
        in_specs=[pl.BlockSpec((None, pages, PAGE, H), lambda i: (layer, i, 0, 0))],
        out_specs=[spec, spec],
        compiler_params=_params(("parallel",), 6 * pages * PAGE * LANES * 4),
        name="fox_page_suffix",
    )(cache_logf)


def _logsig_kernel(x_ref, b_ref, o_ref):
    o_ref[...] = _log_sigmoid(x_ref[...] + b_ref[...])


def fox_logf_step(fl, b_f):
    return pl.pallas_call(
        _logsig_kernel, out_shape=jax.ShapeDtypeStruct(fl.shape, F32), name="fox_logf_step",
    )(fl, b_f.reshape(1, -1).astype(F32))


def _fox_decode_kernel(pt_ref, *refs, n_pages, G, H):
    k_refs = refs[0:G]
    v_refs = refs[G:2 * G]
    w_refs = refs[2 * G:3 * G]
    t_refs = refs[3 * G:4 * G]
    q_ref, kn_ref, vn_ref, lfn_ref, lfc_ref, o_ref, m_scr, l_scr, acc_scr, carry_scr = refs[4 * G:]
    g = pl.program_id(1)
    scale = HEAD ** -0.5
    W = PAGE * H
    q = q_ref[...]
    qb = q.astype(BF16)
    lane = lax.broadcasted_iota(jnp.int32, (H, W), 1)
    lane_head = jnp.bitwise_and(lane, H - 1) if H & (H - 1) == 0 else lane % H
    own = lane_head == lax.broadcasted_iota(jnp.int32, (H, W), 0)

    @pl.when(g == 0)
    def _():
        m_scr[...] = jnp.full_like(m_scr, -jnp.inf)
        l_scr[...] = jnp.zeros_like(l_scr)
        acc_scr[...] = jnp.zeros_like(acc_scr)
        carry_scr[...] = jnp.zeros_like(carry_scr)

    def online_update(s, pv):
        m_old = m_scr[...]
        m_new = jnp.maximum(m_old, jnp.max(s, axis=-1, keepdims=True))
        alpha = jnp.exp(m_old - m_new)
        p = jnp.exp(s - m_new)
        l_scr[...] = l_scr[...] * alpha + jnp.sum(p, axis=-1, keepdims=True)
        acc_scr[...] = acc_scr[...] * alpha + pv(p)
        m_scr[...] = m_new

    for i in range(G):
        kb = k_refs[i][...].reshape(W, HEAD).astype(BF16)
        vb = v_refs[i][...].reshape(W, HEAD).astype(BF16)
        bias = lfn_ref[...] + (w_refs[i][...] + carry_scr[...])
        carry_scr[...] = carry_scr[...] + t_refs[i][...]
        s = jnp.where(own, _dot_nt(qb, kb) * scale + bias, -jnp.inf)
        online_update(s, lambda p: _dot(p.astype(BF16), vb))

    @pl.when(g == pl.num_programs(1) - 1)
    def _():
        cum_new = lfc_ref[...]
        s_new = jnp.sum(q * kn_ref[...], axis=-1, keepdims=True) * scale + (cum_new - cum_new)
        online_update(s_new, lambda p: p * vn_ref[...])
        o_ref[...] = (acc_scr[...] / l_scr[...]).astype(o_ref.dtype)


def fox_decode(q, k_new, v_new, fl_new, b_f, cache_k, cache_v, cache_logf, layer, page_table, D):
    Bd = q.shape[0]
    H = D // HEAD
    n_pages = page_table.shape[1]
    G = 4 if n_pages % 4 == 0 else 1
    W = PAGE * H
    lf_new = fox_logf_step(fl_new, b_f)
    within, total = fox_page_suffix(cache_logf, layer)
    within = within.reshape(-1, 1, W)
    total = total.reshape(-1, 1, W)

    def page_spec(block):
        def make(i):
            def index(b, g, pt):
                return block[0](pt[b, n_pages - 1 - (g * G + i)])
            return pl.BlockSpec(block[1], index)
        return [make(i) for i in range(G)]

    kv = ((lambda p: (layer, p, 0, 0, 0)), (None, None, PAGE, H, HEAD))
    flat = ((lambda p: (p, 0, 0)), (None, 1, W))
    per_seq = lambda shape: pl.BlockSpec((None,) + shape, lambda b, g, pt: (b, 0, 0))
    vmem = 2 * 2 * G * PAGE * D * 4 + 2 * G * PAGE * D * 2 + 4 * G * 8 * W * 4 + 10 * H * W * 4
    grid_spec = pltpu.PrefetchScalarGridSpec(
        num_scalar_prefetch=1,
        grid=(Bd, n_pages // G),
        in_specs=(page_spec(kv) + page_spec(kv) + page_spec(flat) + page_spec(flat)
                  + [per_seq((H, HEAD))] * 3 + [per_seq((1, W)), per_seq((H, 1))]),
        out_specs=per_seq((H, HEAD)),
        scratch_shapes=[pltpu.VMEM((H, 1), F32), pltpu.VMEM((H, 1), F32), pltpu.VMEM((H, HEAD), F32),
                        pltpu.VMEM((1, W), F32)],
    )
    heads = lambda x: x.reshape(Bd, H, HEAD)
    o = pl.pallas_call(
        functools.partial(_fox_decode_kernel, n_pages=n_pages, G=G, H=H),
        out_shape=jax.ShapeDtypeStruct((Bd, H, HEAD), BF16),
        grid_spec=grid_spec,
        compiler_params=_params(("parallel", "arbitrary"), vmem),
        name="fox_decode",
    )(page_table, *([cache_k] * G), *([cache_v] * G), *([within] * G), *([total] * G),
      heads(q), heads(k_new), heads(v_new), jnp.tile(lf_new, (1, PAGE)).reshape(Bd, 1, W),
      lf_new.reshape(Bd, H, 1))
    return o.reshape(Bd, D), lf_new


def _ffn(hp, hs, norm_w, w_gu, w_down, layer):
    act_p, act_s = matmul_swiglu(rmsnorm(hp, norm_w, BF16), w_gu, layer, sample=(rmsnorm(hs, norm_w, BF16), None))
    return matmul(act_p, w_down, layer, residual=hp, sample=(act_s, hs))


def kernel(x_prompt, x_sample, cache_k, cache_v, cache_logf, page_table, state_rec, state_conv,
           meta_tokens, norm_mix, norm_ffn, norm_final, w_in_a, conv_w_a, a_log_a, dt_bias_a,
           onorm_a, w_out_a, w_in_b, b_f, w_out_b, w_gu, w_down):
    B, S, D = x_prompt.shape
    Bd, Q, _ = x_sample.shape
    assert Q == 1 and Bd <= SAMPLE_ROWS, "the sample group is a single-token step of a few sequences"
    H = D // HEAD
    T = S + N_META_TOK
    M = B * T
    meta = jnp.broadcast_to(meta_tokens.astype(x_prompt.dtype)[None], (B, N_META_TOK, D))
    hp = jnp.concatenate([meta, x_prompt], axis=1).reshape(M, D)
    pad = lambda a: jnp.pad(a, ((0, SAMPLE_ROWS - a.shape[0]), (0, 0)))
    hs = pad(x_sample.reshape(Bd, D))
    bf = lambda w: w.astype(BF16)
    w_in_a, w_out_a, w_in_b, w_out_b, w_gu, w_down = map(bf, (w_in_a, w_out_a, w_in_b, w_out_b, w_gu, w_down))

    w_ab = w_in_a[:, :, 4 * D:]
    ap = rmsnorm(hp, norm_mix[0], BF16)
    a_s = rmsnorm(hs, norm_mix[0], BF16)
    proj_p, proj_s = matmul(ap, w_in_a, cols=(0, 4 * D), sample=(a_s, None))
    ab_p, ab_s = matmul(ap, w_ab, sample=(a_s, None))
    proj_p = proj_p.reshape(B, T, 4 * D)
    proj_s = proj_s[:Bd]

    qkv_c = gdn_conv_prompt(proj_p, conv_w_a[0], D)
    on_p, rec_p = gdn_prompt(qkv_c, proj_p, ab_p.reshape(B, T, 2 * H), a_log_a[0], dt_bias_a[0], onorm_a[0], D)
    conv_p = proj_p[:, T - (CONV_TAPS - 1):, :3 * D]
    on_s, rec_s = gdn_sample(proj_s, ab_s[:Bd], state_rec[0], state_conv[0], conv_w_a[0], a_log_a[0],
                             dt_bias_a[0], onorm_a[0], D)
    conv_s = jnp.concatenate([state_conv[0][:, 1:], proj_s[:, None, :3 * D].astype(state_conv.dtype)], axis=1)

    hp, hs = matmul(on_p.reshape(M, D), w_out_a, residual=hp, sample=(pad(on_s.reshape(Bd, D)), hs))
    hp, hs = _ffn(hp, hs, norm_ffn[0], w_gu, w_down, 0)

    w_f = w_in_b[:, :, 3 * D:]
    ap = rmsnorm(hp, norm_mix[1], BF16)
    a_s = rmsnorm(hs, norm_mix[1], BF16)
    (q_p, q_s), (k_p, k_s), (v_p, v_s) = (matmul(ap, w_in_b, cols=(part * D, D), sample=(a_s, None))
                                          for part in range(3))
    q_p, k_p, v_p = (x.reshape(B, T, D) for x in (q_p, k_p, v_p))
    q_s, k_s, v_s = (x[:Bd] for x in (q_s, k_s, v_s))
    fl_p, fl_s = matmul(ap, w_f, sample=(a_s, None))

    lf_p, cum_p = fox_logf(fl_p.reshape(B, T, H), b_f[0])
    o_p = fox_prompt(q_p, k_p, v_p, cum_p)
    o_s, lf_s = fox_decode(q_s, k_s, v_s, fl_s[:Bd], b_f[0], cache_k, cache_v, cache_logf, 0, page_table, D)

    hp, hs = matmul(o_p.reshape(M, D), w_out_b, residual=hp, sample=(pad(o_s), hs))
    hp, hs = _ffn(hp, hs, norm_ffn[1], w_gu, w_down, 1)

    y_prompt = rmsnorm(hp, norm_final, F32).reshape(B, T, D)[:, N_META_TOK:]
    y_sample = rmsnorm(hs, norm_final, F32)[:Bd].reshape(Bd, 1, D)

    k_prompt = k_p.reshape(1, B, T, H, HEAD)
    v_prompt = v_p.reshape(1, B, T, H, HEAD)
    logf_prompt = lf_p.astype(cache_logf.dtype)[None]
    k_sample = k_s.reshape(1, Bd, 1, H, HEAD)
    v_sample = v_s.reshape(1, Bd, 1, H, HEAD)
    logf_sample = lf_s.astype(cache_logf.dtype).reshape(1, Bd, 1, H)
    return (y_prompt, y_sample, k_prompt, v_prompt, logf_prompt, k_sample, v_sample, logf_sample,
            rec_p[None], conv_p[None], rec_s.astype(state_rec.dtype)[None], conv_s[None])
```

```python
import functools
import math

import jax
import jax.numpy as jnp
from jax import lax
from jax.experimental import pallas as pl
from jax.experimental.pallas import tpu as pltpu

F32 = jnp.float32
BF16 = jnp.bfloat16
HI = lax.Precision.HIGHEST

HEAD = 128
N_META_TOK = 16
CONV_TAPS = 4
GDN_CHUNK = 64
PAGE = 128
EPS = 1e-6

VMEM_BYTES_V7X = 64 << 20
VMEM_LIMIT_MAX = VMEM_BYTES_V7X - (8 << 20)
BF16_SUBLANES = 16
LANES = 128


def _dot(a, b, precision=None):
    return jnp.dot(a, b, preferred_element_type=F32, precision=precision)


def _dot_nt(a, b, precision=None):
    return lax.dot_general(a, b, (((1,), (1,)), ((), ())), preferred_element_type=F32,
                           precision=precision)


def _dot_tn(a, b, precision=None):
    return lax.dot_general(a, b, (((0,), (0,)), ((), ())), preferred_element_type=F32,
                           precision=precision)


def _split2(x):
    hi = x.astype(BF16)
    return hi, (x - hi.astype(F32)).astype(BF16)


def _split3(x):
    hi = x.astype(BF16)
    r = x - hi.astype(F32)
    mid = r.astype(BF16)
    return hi, mid, (r - mid.astype(F32)).astype(BF16)


def _dot_sel(sel, x, dot=_dot):
    hi, mid, lo = _split3(x)
    return (dot(sel, hi) + dot(sel, mid)) + dot(sel, lo)


def _dot_sel_r(x, sel, dot=_dot):
    hi, mid, lo = _split3(x)
    return (dot(hi, sel) + dot(mid, sel)) + dot(lo, sel)


def _dot_x3(a, b):
    ah, al = _split2(a)
    bh, bl = _split2(b)
    return _dot(ah, bh) + (_dot(ah, bl) + _dot(al, bh))


def _sigmoid(x):
    return 1.0 / (1.0 + jnp.exp(-x))


def _silu(x):
    return x * _sigmoid(x)


def _softplus(x):
    return jnp.maximum(x, 0.0) + jnp.log1p(jnp.exp(-jnp.abs(x)))


def _log_sigmoid(x):
    return -_softplus(-x)


def _divisor(n, cap, align):
    best = None
    for d in range(align, min(n, cap) + 1, align):
        if n % d == 0:
            best = d
    return n if best is None else best


def _params(semantics, vmem_bytes):
    limit = int(min(max(vmem_bytes * 5 // 4 + (4 << 20), 32 << 20), VMEM_LIMIT_MAX))
    return pltpu.CompilerParams(dimension_semantics=semantics, vmem_limit_bytes=limit)


def _rmsnorm_kernel(x_ref, w_ref, o_ref):
    x = x_ref[...]
    y = x * lax.rsqrt(jnp.mean(x * x, axis=-1, keepdims=True) + EPS)
    o_ref[...] = (y * w_ref[...]).astype(o_ref.dtype)


def rmsnorm(x, w, out_dtype):
    M, D = x.shape
    bm = _divisor(M, max(BF16_SUBLANES, (3 << 20) // (4 * D)), BF16_SUBLANES)
    vmem = 2 * bm * D * (4 + jnp.dtype(out_dtype).itemsize)
    return pl.pallas_call(
        _rmsnorm_kernel,
        out_shape=jax.ShapeDtypeStruct((M, D), out_dtype),
        grid=(M // bm,),
        in_specs=[pl.BlockSpec((bm, D), lambda i: (i, 0)),
                  pl.BlockSpec((1, D), lambda i: (0, 0))],
        out_specs=pl.BlockSpec((bm, D), lambda i: (i, 0)),
        compiler_params=_params(("parallel",), vmem),
        name="rmsnorm",
    )(x, w.reshape(1, D).astype(F32))


SAMPLE_ROWS = BF16_SUBLANES


def _mm_body(n_weights, has_residual, has_sample):
    def body(*refs):
        refs = list(refs)
        x_ref = refs.pop(0)
        w_refs = [refs.pop(0) for _ in range(n_weights)]
        r_ref = refs.pop(0) if has_residual else None
        xs_ref = refs.pop(0) if has_sample else None
        rs_ref = refs.pop(0) if has_sample and has_residual else None
        o_ref = refs.pop(0)
        os_ref = refs.pop(0) if has_sample else None

        def compute(x_r, res_r, out_r):
            ys = [_dot(x_r[...], w_r[...].astype(BF16)) for w_r in w_refs]
            y = ys[0] if n_weights == 1 else _silu(ys[0]) * ys[1]
            out_r[...] = (y if res_r is None else res_r[...] + y).astype(out_r.dtype)

        compute(x_ref, r_ref, o_ref)
        if has_sample:
            @pl.when(pl.program_id(0) == 0)
            def _():
                compute(xs_ref, rs_ref, os_ref)

            @pl.when(pl.program_id(0) != 0)
            def _():
                os_ref[...] = jnp.zeros_like(os_ref)

    return body


MM_VMEM_BUDGET = 48 << 20


def _mm_vmem(bm, bn, K, n_weights, out_bytes, residual, w_bytes=2):
    x_bufs, cast = (2, 0) if w_bytes == 2 else (1, n_weights * K * bn * 2)
    return (x_bufs * bm * K * 2 + 2 * n_weights * K * bn * w_bytes + cast + 2 * bm * bn * out_bytes
            + (2 + n_weights) * bm * bn * 4 + (2 * bm * bn * 4 if residual else 0))


def _mm_tiles(M, K, N, n_weights=1, out_bytes=4, residual=False, w_bytes=2):
    bm = _divisor(M, max(BF16_SUBLANES, (16 << 20) // (2 * K)), BF16_SUBLANES)
    if N % LANES:
        return bm, N
    bn = LANES
    for cand in (4 * LANES, 2 * LANES):
        if N % cand == 0 and _mm_vmem(bm, cand, K, n_weights, out_bytes, residual, w_bytes) <= MM_VMEM_BUDGET:
            bn = cand
            break
    return bm, bn


def _matmul(x, w, layer, col_blocks, N, residual, out_dtype, sample, name):
    M, K = x.shape
    n_weights = len(col_blocks(LANES))
    out_bytes = jnp.dtype(out_dtype).itemsize
    w_bytes = jnp.dtype(w.dtype).itemsize
    bm, bn = _mm_tiles(M, K, N, n_weights, out_bytes, residual is not None, w_bytes)
    tile = pl.BlockSpec((bm, bn), lambda i, j: (i, j))
    x_mode = {} if w_bytes == 2 else {"pipeline_mode": pl.Buffered(1)}
    in_specs = [pl.BlockSpec((bm, K), lambda i, j: (i, 0), **x_mode)]
    in_specs += [pl.BlockSpec((None, K, bn), functools.partial(lambda i, j, j0: (layer, 0, j + j0), j0=j0))
                 for j0 in col_blocks(bn)]
    args = [x] + [w] * n_weights
    out_shape = [jax.ShapeDtypeStruct((M, N), out_dtype)]
    out_specs = [tile]
    if residual is not None:
        in_specs.append(tile)
        args.append(residual)
    if sample is not None:
        xs, rs = sample
        in_specs.append(pl.BlockSpec((SAMPLE_ROWS, K), lambda i, j: (0, 0)))
        args.append(xs)
        if residual is not None:
            in_specs.append(pl.BlockSpec((SAMPLE_ROWS, bn), lambda i, j: (0, j)))
            args.append(rs)
        out_shape.append(jax.ShapeDtypeStruct((M // bm * SAMPLE_ROWS, N), out_dtype))
        out_specs.append(pl.BlockSpec((SAMPLE_ROWS, bn), lambda i, j: (i, j)))
    outs = pl.pallas_call(
        _mm_body(n_weights, residual is not None, sample is not None),
        out_shape=out_shape,
        grid=(M // bm, N // bn),
        in_specs=in_specs,
        out_specs=out_specs,
        compiler_params=_params(("parallel", "parallel"),
                                _mm_vmem(bm, bn, K, n_weights, out_bytes, residual is not None, w_bytes)),
        name=name,
    )(*args)
    return outs[0] if sample is None else (outs[0], outs[1][:SAMPLE_ROWS])


def matmul(x, w, layer=0, residual=None, out_dtype=F32, cols=None, sample=None):
    col0, N = (0, w.shape[2]) if cols is None else cols
    assert col0 + N <= w.shape[2]

    def col_blocks(bn):
        assert col0 % bn == 0 or N % LANES
        return [col0 // bn]

    return _matmul(x, w, layer, col_blocks, N, residual, out_dtype, sample, "matmul")


def matmul_swiglu(x, w_gu, layer, sample=None):
    F = w_gu.shape[2] // 2
    return _matmul(x, w_gu, layer, lambda bn: [0, F // bn], F, None, BF16, sample, "matmul_swiglu")


def _conv_kernel(x_ref, w_ref, o_ref, *, T, rows, heads_per_block, blocks_per_part):
    kind = pl.program_id(1) // blocks_per_part
    w = w_ref[...]
    scale = jnp.where(kind == 0, HEAD ** -0.5, 1.0).astype(F32)
    halo = 8

    def conv_silu(i):
        r0 = pl.multiple_of(i * rows, rows)
        cur = x_ref[pl.ds(r0, rows), :]
        prev = x_ref[pl.ds(pl.multiple_of(jnp.maximum(r0 - halo, 0), halo), halo), :]
        xx = jnp.concatenate([jnp.where(i > 0, prev, 0.0), cur], axis=0)
        acc = None
        for j in range(CONV_TAPS):
            s = CONV_TAPS - 1 - j
            xs = cur if s == 0 else pltpu.roll(xx, s, 0)[halo:, :]
            term = xs * w[j:j + 1, :]
            acc = term if acc is None else acc + term
        return r0, _silu(acc)

    def qk_chunk(i, carry):
        r0, y = conv_silu(i)
        for h in range(heads_per_block):
            seg = y[:, h * HEAD:(h + 1) * HEAD]
            r = lax.rsqrt(jnp.sum(seg * seg, axis=-1, keepdims=True) + EPS)
            o_ref[pl.ds(r0, rows), h * HEAD:(h + 1) * HEAD] = seg * r * scale
        return carry

    def v_chunk(i, carry):
        r0, y = conv_silu(i)
        o_ref[pl.ds(r0, rows), :] = y
        return carry

    n = T // rows
    unroll = 2

    @pl.when(kind < 2)
    def _():
        lax.fori_loop(0, n, qk_chunk, 0, unroll=unroll)

    @pl.when(kind == 2)
    def _():
        lax.fori_loop(0, n, v_chunk, 0, unroll=unroll)


def gdn_conv_prompt(proj, conv_w, D):
    B, T, _ = proj.shape
    cb = _divisor(D, 512, HEAD)
    rows = _divisor(T, 64, 8)
    vmem = 4 * T * cb * 4
    return pl.pallas_call(
        functools.partial(_conv_kernel, T=T, rows=rows, heads_per_block=cb // HEAD,
                          blocks_per_part=D // cb),
        out_shape=jax.ShapeDtypeStruct((B, T, 3 * D), F32),
        grid=(B, 3 * D // cb),
        in_specs=[pl.BlockSpec((None, T, cb), lambda b, j: (b, 0, j)),
                  pl.BlockSpec((CONV_TAPS, cb), lambda b, j: (0, j))],
        out_specs=pl.BlockSpec((None, T, cb), lambda b, j: (b, 0, j)),
        compiler_params=_params(("parallel", "parallel"), vmem),
        name="gdn_conv",
    )(proj, conv_w)


def _each_dot_sel(sel, xs):
    sel3 = jnp.concatenate([sel, sel, sel], axis=1).astype(BF16)
    stacked = [jnp.concatenate(_split3(x), axis=0) for x in xs]
    return [_dot(sel3, s) for s in stacked]


def _each_dot_x3(as_, bs):
    lhs, rhs = [], []
    for a, b in zip(as_, bs):
        ah = a.astype(BF16).astype(F32)
        lhs.append(jnp.concatenate([ah, ah, a - ah], axis=1).astype(BF16))
        bh, bl = _split2(b)
        rhs.append(jnp.concatenate([bh, bl, bh], axis=0))
    return [_dot(l, r) for l, r in zip(lhs, rhs)]


def _each_neumann_inverse(bms, L):
    eye = (lax.broadcasted_iota(jnp.int32, (L, L), 0) ==
           lax.broadcasted_iota(jnp.int32, (L, L), 1)).astype(F32)
    ps = [-bm for bm in bms]
    ts = [eye + p for p in ps]
    for _ in range(int(math.log2(L)) - 1):
        ps = _each_dot_x3(ps, ps)
        ts = [t + d for t, d in zip(ts, _each_dot_x3(ts, ps))]
    return ts


def _gdn_kernel(alog_ref, dtb_ref, q_ref, k_ref, v_ref, z_ref, ab_ref, onorm_ref,
                o_ref, s_out_ref,
                g_scr, b_scr, w_scr, u_scr, eg_scr, a_scr, c_scr, qk_scr, qkm_scr, s_scr,
                *, T, H, hb, n_meta, chunk, prep_unroll):
    n_chunks = (T - n_meta) // chunk
    heads = [(hh, pl.program_id(1) * hb + hh, slice(hh * HEAD, (hh + 1) * HEAD)) for hh in range(hb)]

    ab_parts = _split3(ab_ref[...])
    sel_row = lax.broadcasted_iota(jnp.int32, (2 * H, 2 * HEAD), 0)
    sel_col = lax.broadcasted_iota(jnp.int32, (2 * H, 2 * HEAD), 1)
    for _, h, lanes in heads:
        sel = (sel_row == jnp.where(sel_col < HEAD, h, h + H)).astype(BF16)
        ab_col = (_dot(ab_parts[0], sel) + _dot(ab_parts[1], sel)) + _dot(ab_parts[2], sel)
        decay_rate = jnp.exp(jnp.full((1, HEAD), alog_ref[h], F32))
        g_scr[:, lanes] = -decay_rate * _softplus(ab_col[:, :HEAD] + dtb_ref[h])
        b_scr[:, lanes] = _sigmoid(ab_col[:, HEAD:])

    def prep(offs, L, qk_ref):
        ri = lax.broadcasted_iota(jnp.int32, (L, L), 0)
        ci = lax.broadcasted_iota(jnp.int32, (L, L), 1)
        tri = ri >= ci
        strict = ri > ci
        eye = ri == ci
        jobs = [(pl.ds(off, L), pl.ds(qk_off, L), hh, lanes) for off, qk_off, _ in offs for hh, _, lanes in heads]
        slots = [slot for _, _, slot in offs for _ in heads]
        ks = [k_ref[rows, lanes] for rows, _, _, lanes in jobs]
        betas = [b_scr[rows, lanes] for rows, _, _, lanes in jobs]
        gcs = _each_dot_sel(tri.astype(F32), [g_scr[rows, lanes] for rows, _, _, lanes in jobs])
        gc_is = [gc[:, :L] for gc in gcs]
        gc_js = _each_dot_sel(jnp.ones((L, L), F32), [jnp.where(eye, g, 0.0) for g in gc_is])
        decays = [jnp.where(tri, jnp.exp(jnp.where(tri, gi - gj, 0.0)), 0.0) for gi, gj in zip(gc_is, gc_js)]
        kbs = [k.astype(BF16) for k in ks]
        kks = [_dot_nt(kb, kb) for kb in kbs]
        bms = [jnp.where(strict, kk * d, 0.0) * beta[:, :L] for kk, d, beta in zip(kks, decays, betas)]
        ts = _each_neumann_inverse(bms, L)
        egs = [jnp.exp(gc) for gc in gcs]
        wus = _each_dot_x3(ts, [jnp.concatenate([beta * eg * k, beta * v_ref[rows, lanes]], axis=1)
                                for beta, eg, k, (rows, _, _, lanes) in zip(betas, egs, ks, jobs)])
        ws = [wu[:, :HEAD] for wu in wus]
        us = [wu[:, HEAD:] for wu in wus]
        qks = [_dot_nt(q_ref[rows, lanes].astype(BF16), kb) * d
               for (rows, _, _, lanes), kb, d in zip(jobs, kbs, decays)]
        kts = [(k * jnp.exp(gc[L - 1:L, :] - gc)).astype(BF16) for k, gc in zip(ks, gcs)]
        acs = [_dot_tn(kt, wu.astype(BF16)) for kt, wu in zip(kts, wus)]
        for (rows, qk_rows, hh, lanes), slot, w, u, eg, ac, qk in zip(jobs, slots, ws, us, egs, acs, qks):
            w_scr[rows, lanes] = w
            u_scr[rows, lanes] = u
            eg_scr[rows, lanes] = eg
            a_scr[hh, slot] = ac[:, :HEAD].astype(BF16)
            c_scr[hh, slot] = ac[:, HEAD:]
            qk_ref[hh, qk_rows, :] = qk

    def advance(off, L, slot):
        rows = pl.ds(off, L)
        ss = [s_scr[hh] for hh, _, _ in heads]
        lhs = [jnp.concatenate([a_scr[hh, slot], w_scr[rows, lanes].astype(BF16),
                                q_ref[rows, lanes].astype(BF16)], axis=0) for hh, _, lanes in heads]
        prods = [_dot(l, s.astype(BF16)) for l, s in zip(lhs, ss)]
        for (hh, _, lanes), s, p in zip(heads, ss, prods):
            eg_last = eg_scr[pl.ds(off + L - 8, 8), lanes][7:8, :]
            s_scr[hh] = (eg_last * s - p[:HEAD]) + c_scr[hh, slot]
        return tuple(x for (_, _, lanes), p in zip(heads, prods)
                     for x in (u_scr[rows, lanes] - p[HEAD:HEAD + L], p[HEAD + L:]))

    def emit(off, L, qk_ref, qk_off, carried):
        rows = pl.ds(off, L)
        for i, (hh, _, lanes) in enumerate(heads):
            u, qs = carried[2 * i], carried[2 * i + 1]
            o = eg_scr[rows, lanes] * qs + _dot(qk_ref[hh, pl.ds(qk_off, L), :].astype(BF16), u.astype(BF16))
            on = o * lax.rsqrt(jnp.mean(o * o, axis=-1, keepdims=True) + EPS) * onorm_ref[...]
            o_ref[rows, lanes] = (on * _silu(z_ref[rows, lanes])).astype(o_ref.dtype)

    def chunk_off(c):
        off = n_meta + c * chunk
        return off if isinstance(c, int) else pl.multiple_of(off, BF16_SUBLANES)

    def chunk_qk_off(c):
        return c * chunk if isinstance(c, int) else pl.multiple_of(c * chunk, chunk)

    prep([(0, 0, 0)], n_meta, qkm_scr)

    def prep_body(i, carry):
        cs = [i * prep_unroll + j for j in range(prep_unroll)]
        prep([(chunk_off(c), chunk_qk_off(c), c + 1) for c in cs], chunk, qk_scr)
        return carry

    lax.fori_loop(0, n_chunks // prep_unroll, prep_body, 0)

    s_scr[...] = jnp.zeros_like(s_scr)
    emit(0, n_meta, qkm_scr, 0, advance(0, n_meta, 0))

    def step(c, carried):
        new = advance(chunk_off(c), chunk, c + 1)
        emit(chunk_off(c - 1), chunk, qk_scr, chunk_qk_off(c - 1), carried)
        return new

    last = lax.fori_loop(1, n_chunks, step, advance(chunk_off(0), chunk, 1))
    emit(chunk_off(n_chunks - 1), chunk, qk_scr, chunk_qk_off(n_chunks - 1), last)
    s_out_ref[...] = s_scr[...]


def gdn_prompt(qkv_c, proj, ab, a_log, dt_bias, onorm, D):
    B, T, _ = qkv_c.shape
    H = D // HEAD
    assert (T - N_META_TOK) % GDN_CHUNK == 0
    n_chunks = (T - N_META_TOK) // GDN_CHUNK
    hb = 2 if H % 2 == 0 else 1
    prep_unroll = _divisor(n_chunks, 8, 1)
    W = hb * HEAD
    col = lambda part: pl.BlockSpec((None, T, W), lambda b, g: (b, 0, part * (H // hb) + g))
    smem = pl.BlockSpec(memory_space=pltpu.SMEM)
    vmem = ((8 + 5) * T * W * 4 + 2 * T * LANES * 4 + 2 * T * W * 2 + hb * n_chunks * GDN_CHUNK * LANES * 4
            + hb * (n_chunks + 1) * HEAD * HEAD * 6)
    return pl.pallas_call(
        functools.partial(_gdn_kernel, T=T, H=H, hb=hb, n_meta=N_META_TOK, chunk=GDN_CHUNK,
                          prep_unroll=prep_unroll),
        out_shape=[jax.ShapeDtypeStruct((B, T, D), BF16),
                   jax.ShapeDtypeStruct((B, H, HEAD, HEAD), F32)],
        grid=(B, H // hb),
        in_specs=[smem, smem, col(0), col(1), col(2), col(3),
                  pl.BlockSpec((None, T, 2 * H), lambda b, g: (b, 0, 0)),
                  pl.BlockSpec((1, HEAD), lambda b, g: (0, 0))],
        out_specs=[pl.BlockSpec((None, T, W), lambda b, g: (b, 0, g)),
                   pl.BlockSpec((None, hb, HEAD, HEAD), lambda b, g: (b, g, 0, 0))],
        scratch_shapes=[pltpu.VMEM((T, W), F32)] * 5 + [
            pltpu.VMEM((hb, n_chunks + 1, HEAD, HEAD), BF16),
            pltpu.VMEM((hb, n_chunks + 1, HEAD, HEAD), F32),
            pltpu.VMEM((hb, n_chunks * GDN_CHUNK, GDN_CHUNK), F32),
            pltpu.VMEM((hb, N_META_TOK, N_META_TOK), F32),
            pltpu.VMEM((hb, HEAD, HEAD), F32)],
        compiler_params=_params(("parallel", "parallel"), vmem),
        name="gdn_chunk",
    )(a_log.astype(F32), dt_bias.astype(F32), qkv_c, qkv_c, qkv_c, proj, ab, onorm.reshape(1, HEAD).astype(F32))


def _gdn_step_kernel(u_ref, cs_ref, w_ref, ab_ref, alog_ref, dtb_ref, onorm_ref, s_ref,
                     o_ref, s_out_ref, q_scr, k_scr, v_scr, g_scr, b_scr, o_scr, *, H):
    u_all = u_ref[...]
    u = u_all[:3 * H]
    xc = (cs_ref[0] * w_ref[0] + cs_ref[1] * w_ref[1] + cs_ref[2] * w_ref[2]) + u * w_ref[3]
    xc = _silu(xc)
    q = xc[:H]
    k = xc[H:2 * H]
    q_scr[...] = q * lax.rsqrt(jnp.sum(q * q, axis=-1, keepdims=True) + EPS) * (HEAD ** -0.5)
    k_scr[...] = k * lax.rsqrt(jnp.sum(k * k, axis=-1, keepdims=True) + EPS)
    v_scr[...] = xc[2 * H:]
    g_scr[...] = jnp.broadcast_to(-jnp.exp(alog_ref[...]) * _softplus(ab_ref[0] + dtb_ref[...]), (H, HEAD))
    b_scr[...] = jnp.broadcast_to(_sigmoid(ab_ref[1]), (H, HEAD))
    eye = (lax.broadcasted_iota(jnp.int32, (HEAD, HEAD), 0) ==
           lax.broadcasted_iota(jnp.int32, (HEAD, HEAD), 1)).astype(F32)

    group = 8 if H % 8 == 0 else 1

    def heads_step(gi, carry):
        hs = [gi * group + j for j in range(group)]
        rows = [pl.ds(h, 1) for h in hs]
        k_rows = [k_scr[r, :] for r in rows]
        ss = [s_ref[h] * jnp.exp(g_scr[r, :]) for h, r in zip(hs, rows)]
        kss = [_dot(k_row, s, HI) for k_row, s in zip(k_rows, ss)]
        upds = [b_scr[r, :] * (v_scr[r, :] - ks) for r, ks in zip(rows, kss)]
        k_cols = [_dot_nt(eye, jnp.broadcast_to(k_row, (HEAD, HEAD)), HI) for k_row in k_rows]
        ss = [s + k_col * upd for s, k_col, upd in zip(ss, k_cols, upds)]
        outs = [_dot(q_scr[r, :], s, HI) for r, s in zip(rows, ss)]
        for h, r, s, o in zip(hs, rows, ss, outs):
            s_out_ref[h] = s
            o_scr[r, :] = o
        return carry

    lax.fori_loop(0, H // group, heads_step, 0)
    o = o_scr[...]
    on = o * lax.rsqrt(jnp.mean(o * o, axis=-1, keepdims=True) + EPS) * onorm_ref[...]
    o_ref[...] = (on * _silu(u_all[3 * H:])).astype(o_ref.dtype)


def gdn_sample(proj, ab, state_rec, state_conv, conv_w, a_log, dt_bias, onorm, D):
    Bd = proj.shape[0]
    H = D // HEAD
    col = lambda x: x.astype(F32).reshape(H, 1)
    vmem = 2 * (4 * H * HEAD * 4 + 3 * 3 * H * HEAD * 4 + 2 * H * HEAD * HEAD * 4) + 4 * 3 * H * HEAD * 4
    return pl.pallas_call(
        functools.partial(_gdn_step_kernel, H=H),
        out_shape=[jax.ShapeDtypeStruct((Bd, H, HEAD), BF16),
                   jax.ShapeDtypeStruct((Bd, H, HEAD, HEAD), F32)],
        grid=(Bd,),
        in_specs=[pl.BlockSpec((None, 4 * H, HEAD), lambda b: (b, 0, 0)),
                  pl.BlockSpec((None, CONV_TAPS - 1, 3 * H, HEAD), lambda b: (b, 0, 0, 0)),
                  pl.BlockSpec((CONV_TAPS, 3 * H, HEAD), lambda b: (0, 0, 0)),
                  pl.BlockSpec((None, 2, H, 1), lambda b: (b, 0, 0, 0)),
                  pl.BlockSpec((H, 1), lambda b: (0, 0)),
                  pl.BlockSpec((H, 1), lambda b: (0, 0)),
                  pl.BlockSpec((1, HEAD), lambda b: (0, 0)),
                  pl.BlockSpec((None, H, HEAD, HEAD), lambda b: (b, 0, 0, 0))],
        out_specs=[pl.BlockSpec((None, H, HEAD), lambda b: (b, 0, 0)),
                   pl.BlockSpec((None, H, HEAD, HEAD), lambda b: (b, 0, 0, 0))],
        scratch_shapes=[pltpu.VMEM((H, HEAD), F32)] * 6,
        compiler_params=_params(("parallel",), vmem),
        name="gdn_step",
    )(proj.reshape(Bd, 4 * H, HEAD), state_conv.reshape(Bd, CONV_TAPS - 1, 3 * H, HEAD),
      conv_w.reshape(CONV_TAPS, 3 * H, HEAD), ab.reshape(Bd, 2, H, 1), col(a_log), col(dt_bias),
      onorm.reshape(1, HEAD).astype(F32), state_rec)


def _logf_kernel(fl_ref, bf_ref, lf_ref, cum_ref, *, T):
    carry = jnp.zeros((1, fl_ref.shape[-1]), F32)
    for r0 in range(0, T, LANES):
        n = min(LANES, T - r0)
        lf = _log_sigmoid(fl_ref[r0:r0 + n, :] + bf_ref[...])
        lf_ref[r0:r0 + n, :] = lf
        tri = (lax.broadcasted_iota(jnp.int32, (n, n), 0) >=
               lax.broadcasted_iota(jnp.int32, (n, n), 1)).astype(F32)
        cum = _dot(tri, lf, HI) + carry
        cum_ref[r0:r0 + n, :] = cum
        carry = cum[n - 1:n, :]


def fox_logf(fl, b_f):
    B, T, H = fl.shape
    spec = pl.BlockSpec((None, T, H), lambda b: (b, 0, 0))
    return pl.pallas_call(
        functools.partial(_logf_kernel, T=T),
        out_shape=[jax.ShapeDtypeStruct((B, T, H), F32)] * 2,
        grid=(B,),
        in_specs=[spec, pl.BlockSpec((1, H), lambda b: (0, 0))],
        out_specs=[spec, spec],
        compiler_params=_params(("parallel",), 6 * T * LANES * 4),
        name="fox_logf",
    )(fl, b_f.reshape(1, H).astype(F32))


def _fox_kernel(q_ref, k_ref, v_ref, cum_ref, o_ref, kb_scr, vb_scr, cq_scr, ckm_scr, ckt_scr,
                *, T, H, bq):
    h = pl.program_id(1)
    t_main = (T // bq) * bq
    tail = T - t_main
    scale = HEAD ** -0.5
    kb_scr[...] = k_ref[...].astype(BF16)
    vb_scr[...] = v_ref[...].astype(BF16)
    cum = cum_ref[...]
    cq_scr[...] = _dot_sel_r(cum, (lax.broadcasted_iota(jnp.int32, (H, HEAD), 0) == h).astype(BF16))
    pick = (lax.broadcasted_iota(jnp.int32, (8, H), 1) == h).astype(BF16)
    ckm_scr[...] = _dot_sel(pick, cum[:t_main], _dot_nt)
    if tail:
        ckt_scr[...] = _dot_sel(pick, cum[t_main:], _dot_nt)

    def attend(r0, nq, ck_diag):
        rows = slice(r0, r0 + nq)
        qb = q_ref[rows, :].astype(BF16)
        cq = cq_scr[rows, 0:1]
        causal = (lax.broadcasted_iota(jnp.int32, (nq, nq), 0) >=
                  lax.broadcasted_iota(jnp.int32, (nq, nq), 1))
        s_d = jnp.where(causal, _dot_nt(qb, kb_scr[rows, :]) * scale + cq - ck_diag, -jnp.inf)
        m = jnp.max(s_d, axis=-1, keepdims=True)
        if r0:
            s_o = _dot_nt(qb, kb_scr[:r0, :]) * scale + cq - ckm_scr[0:1, :r0]
            m = jnp.maximum(m, jnp.max(s_o, axis=-1, keepdims=True))
        e_d = jnp.exp(s_d - m)
        l = jnp.sum(e_d, axis=-1, keepdims=True)
        o = _dot(e_d.astype(BF16), vb_scr[rows, :])
        if r0:
            e_o = jnp.exp(s_o - m)
            l = l + jnp.sum(e_o, axis=-1, keepdims=True)
            o = o + _dot(e_o.astype(BF16), vb_scr[:r0, :])
        o_ref[rows, :] = (o / l).astype(o_ref.dtype)

    for r0 in range(0, t_main, bq):
        attend(r0, bq, ckm_scr[0:1, r0:r0 + bq])
    if tail:
        attend(t_main, tail, ckt_scr[0:1, :])


def fox_prompt(q, k, v, cum):
    B, T, D = q.shape
    H = D // HEAD
    bq = 2 * LANES
    t_main = (T // bq) * bq
    tail = T - t_main
    assert tail % BF16_SUBLANES == 0
    head = pl.BlockSpec((None, T, HEAD), lambda b, h: (b, 0, h))
    vmem = 6 * T * HEAD * 4 + 2 * T * LANES * 4 + 2 * T * HEAD * 2 + 2 * T * HEAD * 2 + T * HEAD * 4 \
        + 8 * T * 4 + 6 * bq * t_main * 4
    return pl.pallas_call(
        functools.partial(_fox_kernel, T=T, H=H, bq=bq),
        out_shape=jax.ShapeDtypeStruct((B, T, D), BF16),
        grid=(B, H),
        in_specs=[head, head, head, pl.BlockSpec((None, T, H), lambda b, h: (b, 0, 0))],
        out_specs=head,
        scratch_shapes=[pltpu.VMEM((T, HEAD), BF16), pltpu.VMEM((T, HEAD), BF16),
                        pltpu.VMEM((T, HEAD), F32), pltpu.VMEM((8, t_main), F32),
                        pltpu.VMEM((8, max(tail, 8)), F32)],
        compiler_params=_params(("parallel", "parallel"), vmem),
        name="fox_prompt",
    )(q, k, v, cum)


def _page_suffix_kernel(lf_ref, within_ref, total_ref, *, pages, H):
    x = lf_ref[...]
    W = x.shape[1]
    lane = lax.broadcasted_iota(jnp.int32, x.shape, 1)
    later = x
    earlier = x
    s = H
    while s < W:
        later = later + jnp.where(lane < W - s, pltpu.roll(later, W - s, 1), 0.0)
        earlier = earlier + jnp.where(lane >= s, pltpu.roll(earlier, s, 1), 0.0)
        s *= 2
    within = later - x
    total = within + earlier
    for p in range(pages):
        within_ref[p] = within[p:p + 1, :]
        total_ref[p] = total[p:p + 1, :]


def fox_page_suffix(cache_logf, layer):
    _, n_pool, _, H = cache_logf.shape
    W = PAGE * H
    pages = _divisor(n_pool, 16, 8)
    spec = pl.BlockSpec((pages, 1, W), lambda i: (i, 0, 0))
    return pl.pallas_call(
        functools.partial(_page_suffix_kernel, pages=pages, H=H),
        out_shape=[jax.ShapeDtypeStruct((n_pool, 1, W), F32)] * 2,
        grid=(n_pool // pages,),
        in_specs=[pl.BlockSpec((pages, W), lambda i: (i, 0))],
        out_specs=[spec, spec],
        compiler_params=_params(("parallel",), 12 * pages * W * 4),
        name="fox_page_suffix",
    )(cache_logf[layer].reshape(n_pool, W))


def _logsig_kernel(x_ref, b_ref, o_ref):
    o_ref[...] = _log_sigmoid(x_ref[...] + b_ref[...])


def fox_logf_step(fl, b_f):
    return pl.pallas_call(
        _logsig_kernel, out_shape=jax.ShapeDtypeStruct(fl.shape, F32), name="fox_logf_step",
    )(fl, b_f.reshape(1, -1).astype(F32))


def _fox_decode_kernel(pt_ref, *refs, n_pages, G, H):
    k_refs = refs[0:G]
    v_refs = refs[G:2 * G]
    w_refs = refs[2 * G:3 * G]
    t_refs = refs[3 * G:4 * G]
    q_ref, kn_ref, vn_ref, lfn_ref, lfc_ref, o_ref, m_scr, l_scr, acc_scr, carry_scr = refs[4 * G:]
    g = pl.program_id(1)
    scale = HEAD ** -0.5
    W = PAGE * H
    q = q_ref[...]
    qb = q.astype(BF16)
    lane = lax.broadcasted_iota(jnp.int32, (H, W), 1)
    lane_head = jnp.bitwise_and(lane, H - 1) if H & (H - 1) == 0 else lane % H
    own = lane_head == lax.broadcasted_iota(jnp.int32, (H, W), 0)

    @pl.when(g == 0)
    def _():
        m_scr[...] = jnp.full_like(m_scr, -jnp.inf)
        l_scr[...] = jnp.zeros_like(l_scr)
        acc_scr[...] = jnp.zeros_like(acc_scr)
        carry_scr[...] = jnp.zeros_like(carry_scr)

    def online_update(s, pv):
        m_old = m_scr[...]
        m_new = jnp.maximum(m_old, jnp.max(s, axis=-1, keepdims=True))
        alpha = jnp.exp(m_old - m_new)
        p = jnp.exp(s - m_new)
        l_scr[...] = l_scr[...] * alpha + jnp.sum(p, axis=-1, keepdims=True)
        acc_scr[...] = acc_scr[...] * alpha + pv(p)
        m_scr[...] = m_new

    for i in range(G):
        kb = k_refs[i][...].reshape(W, HEAD).astype(BF16)
        vb = v_refs[i][...].reshape(W, HEAD).astype(BF16)
        bias = lfn_ref[...] + (w_refs[i][...] + carry_scr[...])
        carry_scr[...] = carry_scr[...] + t_refs[i][...]
        s = jnp.where(own, _dot_nt(qb, kb) * scale + bias, -jnp.inf)
        online_update(s, lambda p: _dot(p.astype(BF16), vb))

    @pl.when(g == pl.num_programs(1) - 1)
    def _():
        cum_new = lfc_ref[...]
        s_new = jnp.sum(q * kn_ref[...], axis=-1, keepdims=True) * scale + (cum_new - cum_new)
        online_update(s_new, lambda p: p * vn_ref[...])
        o_ref[...] = (acc_scr[...] / l_scr[...]).astype(o_ref.dtype)


def fox_decode(q, k_new, v_new, fl_new, b_f, cache_k, cache_v, cache_logf, layer, page_table, D):
    Bd = q.shape[0]
    H = D // HEAD
    n_pages = page_table.shape[1]
    G = 4 if n_pages % 4 == 0 else 1
    W = PAGE * H
    lf_new = fox_logf_step(fl_new, b_f)
    within, total = fox_page_suffix(cache_logf, layer)

    def page_spec(block):
        def make(i):
            def index(b, g, pt):
                return block[0](pt[b, n_pages - 1 - (g * G + i)])
            return pl.BlockSpec(block[1], index)
        return [make(i) for i in range(G)]

    kv = ((lambda p: (layer, p, 0, 0, 0)), (None, None, PAGE, H, HEAD))
    flat = ((lambda p: (p, 0, 0)), (None, 1, W))
    per_seq = lambda shape: pl.BlockSpec((None,) + shape, lambda b, g, pt: (b, 0, 0))
    vmem = 2 * 2 * G * PAGE * D * 4 + 2 * G * PAGE * D * 2 + 4 * G * 8 * W * 4 + 10 * H * W * 4
    grid_spec = pltpu.PrefetchScalarGridSpec(
        num_scalar_prefetch=1,
        grid=(Bd, n_pages // G),
        in_specs=(page_spec(kv) + page_spec(kv) + page_spec(flat) + page_spec(flat)
                  + [per_seq((H, HEAD))] * 3 + [per_seq((1, W)), per_seq((H, 1))]),
        out_specs=per_seq((H, HEAD)),
        scratch_shapes=[pltpu.VMEM((H, 1), F32), pltpu.VMEM((H, 1), F32), pltpu.VMEM((H, HEAD), F32),
                        pltpu.VMEM((1, W), F32)],
    )
    heads = lambda x: x.reshape(Bd, H, HEAD)
    o = pl.pallas_call(
        functools.partial(_fox_decode_kernel, n_pages=n_pages, G=G, H=H),
        out_shape=jax.ShapeDtypeStruct((Bd, H, HEAD), BF16),
        grid_spec=grid_spec,
        compiler_params=_params(("parallel", "arbitrary"), vmem),
        name="fox_decode",
    )(page_table, *([cache_k] * G), *([cache_v] * G), *([within] * G), *([total] * G),
      heads(q), heads(k_new), heads(v_new), jnp.tile(lf_new, (1, PAGE)).reshape(Bd, 1, W),
      lf_new.reshape(Bd, H, 1))
    return o.reshape(Bd, D), lf_new


def _ffn(hp, hs, norm_w, w_gu, w_down, layer):
    act_p, act_s = matmul_swiglu(rmsnorm(hp, norm_w, BF16), w_gu, layer, sample=(rmsnorm(hs, norm_w, BF16), None))
    return matmul(act_p, w_down, layer, residual=hp, sample=(act_s, hs))


def kernel(x_prompt, x_sample, cache_k, cache_v, cache_logf, page_table, state_rec, state_conv,
           meta_tokens, norm_mix, norm_ffn, norm_final, w_in_a, conv_w_a, a_log_a, dt_bias_a,
           onorm_a, w_out_a, w_in_b, b_f, w_out_b, w_gu, w_down):
    B, S, D = x_prompt.shape
    Bd, Q, _ = x_sample.shape
    assert Q == 1 and Bd <= SAMPLE_ROWS, "the sample group is a single-token step of a few sequences"
    H = D // HEAD
    T = S + N_META_TOK
    M = B * T
    meta = jnp.broadcast_to(meta_tokens.astype(x_prompt.dtype)[None], (B, N_META_TOK, D))
    hp = jnp.concatenate([meta, x_prompt], axis=1).reshape(M, D)
    pad = lambda a: jnp.pad(a, ((0, SAMPLE_ROWS - a.shape[0]), (0, 0)))
    hs = pad(x_sample.reshape(Bd, D))
    bf = lambda w: w.astype(BF16)
    w_in_a, w_out_a, w_in_b, w_out_b, w_down = map(bf, (w_in_a, w_out_a, w_in_b, w_out_b, w_down))

    w_ab = w_in_a[:, :, 4 * D:]
    ap = rmsnorm(hp, norm_mix[0], BF16)
    a_s = rmsnorm(hs, norm_mix[0], BF16)
    proj_p, proj_s = matmul(ap, w_in_a, cols=(0, 4 * D), sample=(a_s, None))
    ab_p, ab_s = matmul(ap, w_ab, sample=(a_s, None))
    proj_p = proj_p.reshape(B, T, 4 * D)
    proj_s = proj_s[:Bd]

    qkv_c = gdn_conv_prompt(proj_p, conv_w_a[0], D)
    on_p, rec_p = gdn_prompt(qkv_c, proj_p, ab_p.reshape(B, T, 2 * H), a_log_a[0], dt_bias_a[0], onorm_a[0], D)
    conv_p = proj_p[:, T - (CONV_TAPS - 1):, :3 * D]
    on_s, rec_s = gdn_sample(proj_s, ab_s[:Bd], state_rec[0], state_conv[0], conv_w_a[0], a_log_a[0],
                             dt_bias_a[0], onorm_a[0], D)
    conv_s = jnp.concatenate([state_conv[0][:, 1:], proj_s[:, None, :3 * D].astype(state_conv.dtype)], axis=1)

    hp, hs = matmul(on_p.reshape(M, D), w_out_a, residual=hp, sample=(pad(on_s.reshape(Bd, D)), hs))
    hp, hs = _ffn(hp, hs, norm_ffn[0], w_gu, w_down, 0)

    w_f = w_in_b[:, :, 3 * D:]
    ap = rmsnorm(hp, norm_mix[1], BF16)
    a_s = rmsnorm(hs, norm_mix[1], BF16)
    (q_p, q_s), (k_p, k_s), (v_p, v_s) = (matmul(ap, w_in_b, cols=(part * D, D), sample=(a_s, None))
                                          for part in range(3))
    q_p, k_p, v_p = (x.reshape(B, T, D) for x in (q_p, k_p, v_p))
    q_s, k_s, v_s = (x[:Bd] for x in (q_s, k_s, v_s))
    fl_p, fl_s = matmul(ap, w_f, sample=(a_s, None))

    lf_p, cum_p = fox_logf(fl_p.reshape(B, T, H), b_f[0])
    o_p = fox_prompt(q_p, k_p, v_p, cum_p)
    o_s, lf_s = fox_decode(q_s, k_s, v_s, fl_s[:Bd], b_f[0], cache_k, cache_v, cache_logf, 0, page_table, D)

    hp, hs = matmul(o_p.reshape(M, D), w_out_b, residual=hp, sample=(pad(o_s), hs))
    hp, hs = _ffn(hp, hs, norm_ffn[1], w_gu, w_down, 1)

    y_prompt = rmsnorm(hp, norm_final, F32).reshape(B, T, D)[:, N_META_TOK:]
    y_sample = rmsnorm(hs, norm_final, F32)[:Bd].reshape(Bd, 1, D)

    k_prompt = k_p.reshape(1, B, T, H, HEAD)
    v_prompt = v_p.reshape(1, B, T, H, HEAD)
    logf_prompt = lf_p.astype(cache_logf.dtype)[None]
    k_sample = k_s.reshape(1, Bd, 1, H, HEAD)
    v_sample = v_s.reshape(1, Bd, 1, H, HEAD)
    logf_sample = lf_s.astype(cache_logf.dtype).reshape(1, Bd, 1, H)
    return (y_prompt, y_sample, k_prompt, v_prompt, logf_prompt, k_sample, v_sample, logf_sample,
            rec_p[None], conv_p[None], rec_s.astype(state_rec.dtype)[None], conv_s[None])
```

```python
import functools
import math

import jax
import jax.numpy as jnp
from jax import lax
from jax.experimental import pallas as pl
from jax.experimental.pallas import tpu as pltpu

F32 = jnp.float32
BF16 = jnp.bfloat16
HI = lax.Precision.HIGHEST

HEAD = 128
N_META_TOK = 16
CONV_TAPS = 4
GDN_CHUNK = 64
PAGE = 128
EPS = 1e-6

VMEM_BYTES_V7X = 64 << 20
VMEM_LIMIT_MAX = VMEM_BYTES_V7X - (8 << 20)
BF16_SUBLANES = 16
LANES = 128


def _dot(a, b, precision=None):
    return jnp.dot(a, b, preferred_element_type=F32, precision=precision)


def _dot_nt(a, b, precision=None):
    return lax.dot_general(a, b, (((1,), (1,)), ((), ())), preferred_element_type=F32,
                           precision=precision)


def _dot_tn(a, b, precision=None):
    return lax.dot_general(a, b, (((0,), (0,)), ((), ())), preferred_element_type=F32,
                           precision=precision)


def _split2(x):
    hi = x.astype(BF16)
    return hi, (x - hi.astype(F32)).astype(BF16)


def _split3(x):
    hi = x.astype(BF16)
    r = x - hi.astype(F32)
    mid = r.astype(BF16)
    return hi, mid, (r - mid.astype(F32)).astype(BF16)


def _dot_sel(sel, x, dot=_dot):
    hi, mid, lo = _split3(x)
    return (dot(sel, hi) + dot(sel, mid)) + dot(sel, lo)


def _dot_sel_r(x, sel, dot=_dot):
    hi, mid, lo = _split3(x)
    return (dot(hi, sel) + dot(mid, sel)) + dot(lo, sel)


def _dot_x3(a, b):
    ah, al = _split2(a)
    bh, bl = _split2(b)
    return _dot(ah, bh) + (_dot(ah, bl) + _dot(al, bh))


def _sigmoid(x):
    return 1.0 / (1.0 + jnp.exp(-x))


def _silu(x):
    return x * _sigmoid(x)


def _softplus(x):
    return jnp.maximum(x, 0.0) + jnp.log1p(jnp.exp(-jnp.abs(x)))


def _log_sigmoid(x):
    return -_softplus(-x)


def _divisor(n, cap, align):
    best = None
    for d in range(align, min(n, cap) + 1, align):
        if n % d == 0:
            best = d
    return n if best is None else best


def _params(semantics, vmem_bytes):
    limit = int(min(max(vmem_bytes * 5 // 4 + (4 << 20), 32 << 20), VMEM_LIMIT_MAX))
    return pltpu.CompilerParams(dimension_semantics=semantics, vmem_limit_bytes=limit)


def _rmsnorm_kernel(x_ref, w_ref, o_ref):
    x = x_ref[...]
    y = x * lax.rsqrt(jnp.mean(x * x, axis=-1, keepdims=True) + EPS)
    o_ref[...] = (y * w_ref[...]).astype(o_ref.dtype)


def rmsnorm(x, w, out_dtype):
    M, D = x.shape
    bm = _divisor(M, max(BF16_SUBLANES, (3 << 20) // (4 * D)), BF16_SUBLANES)
    vmem = 2 * bm * D * (4 + jnp.dtype(out_dtype).itemsize)
    return pl.pallas_call(
        _rmsnorm_kernel,
        out_shape=jax.ShapeDtypeStruct((M, D), out_dtype),
        grid=(M // bm,),
        in_specs=[pl.BlockSpec((bm, D), lambda i: (i, 0)),
                  pl.BlockSpec((1, D), lambda i: (0, 0))],
        out_specs=pl.BlockSpec((bm, D), lambda i: (i, 0)),
        compiler_params=_params(("parallel",), vmem),
        name="rmsnorm",
    )(x, w.reshape(1, D).astype(F32))


def rmsnorm_tail(x, w, skip):
    B, T, D = x.shape
    S = T - skip
    bs = _divisor(S, max(8, (3 << 20) // (4 * D)), 8)
    def body(x_ref, w_ref, o_ref):
        _rmsnorm_kernel(x_ref.at[0], w_ref, o_ref)

    return pl.pallas_call(
        body,
        out_shape=jax.ShapeDtypeStruct((B, S, D), F32),
        grid=(B, S // bs),
        in_specs=[pl.BlockSpec((pl.Element(1), pl.Element(bs), pl.Element(D)),
                               lambda b, j: (b, pl.multiple_of(skip + j * bs, 8), 0)),
                  pl.BlockSpec((1, D), lambda b, j: (0, 0))],
        out_specs=pl.BlockSpec((None, bs, D), lambda b, j: (b, j, 0)),
        compiler_params=_params(("parallel", "parallel"), 4 * bs * D * 4),
        name="rmsnorm_tail",
    )(x, w.reshape(1, D).astype(F32))


SAMPLE_ROWS = BF16_SUBLANES


def _mm_body(n_weights, has_residual, has_sample):
    def body(*refs):
        refs = list(refs)
        x_ref = refs.pop(0)
        w_refs = [refs.pop(0) for _ in range(n_weights)]
        r_ref = refs.pop(0) if has_residual else None
        xs_ref = refs.pop(0) if has_sample else None
        rs_ref = refs.pop(0) if has_sample and has_residual else None
        o_ref = refs.pop(0)
        os_ref = refs.pop(0) if has_sample else None

        def compute(x_r, res_r, out_r):
            ys = [_dot(x_r[...], w_r[...].astype(BF16)) for w_r in w_refs]
            y = ys[0] if n_weights == 1 else _silu(ys[0]) * ys[1]
            out_r[...] = (y if res_r is None else res_r[...] + y).astype(out_r.dtype)

        compute(x_ref, r_ref, o_ref)
        if has_sample:
            @pl.when(pl.program_id(0) == 0)
            def _():
                compute(xs_ref, rs_ref, os_ref)

            @pl.when(pl.program_id(0) != 0)
            def _():
                os_ref[...] = jnp.zeros_like(os_ref)

    return body


MM_VMEM_BUDGET = 48 << 20


def _mm_vmem(bm, bn, K, n_weights, out_bytes, residual, w_bytes=2):
    x_bufs, cast = (2, 0) if w_bytes == 2 else (1, n_weights * K * bn * 2)
    return (x_bufs * bm * K * 2 + 2 * n_weights * K * bn * w_bytes + cast + 2 * bm * bn * out_bytes
            + (2 + n_weights) * bm * bn * 4 + (2 * bm * bn * 4 if residual else 0))


def _mm_tiles(M, K, N, n_weights=1, out_bytes=4, residual=False, w_bytes=2):
    bm = _divisor(M, max(BF16_SUBLANES, (16 << 20) // (2 * K)), BF16_SUBLANES)
    if N % LANES:
        return bm, N
    bn = LANES
    for cand in (4 * LANES, 2 * LANES):
        if N % cand == 0 and _mm_vmem(bm, cand, K, n_weights, out_bytes, residual, w_bytes) <= MM_VMEM_BUDGET:
            bn = cand
            break
    return bm, bn


def _matmul(x, w, layer, col_blocks, N, residual, out_dtype, sample, name):
    M, K = x.shape
    n_weights = len(col_blocks(LANES))
    out_bytes = jnp.dtype(out_dtype).itemsize
    w_bytes = jnp.dtype(w.dtype).itemsize
    bm, bn = _mm_tiles(M, K, N, n_weights, out_bytes, residual is not None, w_bytes)
    tile = pl.BlockSpec((bm, bn), lambda i, j: (i, j))
    x_mode = {} if w_bytes == 2 else {"pipeline_mode": pl.Buffered(1)}
    in_specs = [pl.BlockSpec((bm, K), lambda i, j: (i, 0), **x_mode)]
    in_specs += [pl.BlockSpec((None, K, bn), functools.partial(lambda i, j, j0: (layer, 0, j + j0), j0=j0))
                 for j0 in col_blocks(bn)]
    args = [x] + [w] * n_weights
    out_shape = [jax.ShapeDtypeStruct((M, N), out_dtype)]
    out_specs = [tile]
    if residual is not None:
        in_specs.append(tile)
        args.append(residual)
    if sample is not None:
        xs, rs = sample
        in_specs.append(pl.BlockSpec((SAMPLE_ROWS, K), lambda i, j: (0, 0)))
        args.append(xs)
        if residual is not None:
            in_specs.append(pl.BlockSpec((SAMPLE_ROWS, bn), lambda i, j: (0, j)))
            args.append(rs)
        out_shape.append(jax.ShapeDtypeStruct((M // bm * SAMPLE_ROWS, N), out_dtype))
        out_specs.append(pl.BlockSpec((SAMPLE_ROWS, bn), lambda i, j: (i, j)))
    outs = pl.pallas_call(
        _mm_body(n_weights, residual is not None, sample is not None),
        out_shape=out_shape,
        grid=(M // bm, N // bn),
        in_specs=in_specs,
        out_specs=out_specs,
        compiler_params=_params(("parallel", "parallel"),
                                _mm_vmem(bm, bn, K, n_weights, out_bytes, residual is not None, w_bytes)),
        name=name,
    )(*args)
    return outs[0] if sample is None else (outs[0], outs[1][:SAMPLE_ROWS])


def matmul(x, w, layer=0, residual=None, out_dtype=F32, cols=None, sample=None):
    col0, N = (0, w.shape[2]) if cols is None else cols
    assert col0 + N <= w.shape[2]

    def col_blocks(bn):
        assert col0 % bn == 0 or N % LANES
        return [col0 // bn]

    return _matmul(x, w, layer, col_blocks, N, residual, out_dtype, sample, "matmul")


def matmul_swiglu(x, w_gu, layer, sample=None):
    F = w_gu.shape[2] // 2
    return _matmul(x, w_gu, layer, lambda bn: [0, F // bn], F, None, BF16, sample, "matmul_swiglu")


def _conv_kernel(x_ref, w_ref, o_ref, *, T, rows, heads_per_block, blocks_per_part):
    kind = pl.program_id(1) // blocks_per_part
    w = w_ref[...]
    scale = jnp.where(kind == 0, HEAD ** -0.5, 1.0).astype(F32)
    halo = 8

    def conv_silu(i):
        r0 = pl.multiple_of(i * rows, rows)
        cur = x_ref[pl.ds(r0, rows), :]
        prev = x_ref[pl.ds(pl.multiple_of(jnp.maximum(r0 - halo, 0), halo), halo), :]
        xx = jnp.concatenate([jnp.where(i > 0, prev, 0.0), cur], axis=0)
        acc = None
        for j in range(CONV_TAPS):
            s = CONV_TAPS - 1 - j
            xs = cur if s == 0 else pltpu.roll(xx, s, 0)[halo:, :]
            term = xs * w[j:j + 1, :]
            acc = term if acc is None else acc + term
        return r0, _silu(acc)

    def qk_chunk(i, carry):
        r0, y = conv_silu(i)
        for h in range(heads_per_block):
            seg = y[:, h * HEAD:(h + 1) * HEAD]
            r = lax.rsqrt(jnp.sum(seg * seg, axis=-1, keepdims=True) + EPS)
            o_ref[pl.ds(r0, rows), h * HEAD:(h + 1) * HEAD] = seg * r * scale
        return carry

    def v_chunk(i, carry):
        r0, y = conv_silu(i)
        o_ref[pl.ds(r0, rows), :] = y
        return carry

    n = T // rows
    unroll = 2

    @pl.when(kind < 2)
    def _():
        lax.fori_loop(0, n, qk_chunk, 0, unroll=unroll)

    @pl.when(kind == 2)
    def _():
        lax.fori_loop(0, n, v_chunk, 0, unroll=unroll)


def gdn_conv_prompt(proj, conv_w, D):
    B, T, _ = proj.shape
    cb = _divisor(D, 512, HEAD)
    rows = _divisor(T, 64, 8)
    vmem = 4 * T * cb * 4
    return pl.pallas_call(
        functools.partial(_conv_kernel, T=T, rows=rows, heads_per_block=cb // HEAD,
                          blocks_per_part=D // cb),
        out_shape=jax.ShapeDtypeStruct((B, T, 3 * D), F32),
        grid=(B, 3 * D // cb),
        in_specs=[pl.BlockSpec((None, T, cb), lambda b, j: (b, 0, j)),
                  pl.BlockSpec((CONV_TAPS, cb), lambda b, j: (0, j))],
        out_specs=pl.BlockSpec((None, T, cb), lambda b, j: (b, 0, j)),
        compiler_params=_params(("parallel", "parallel"), vmem),
        name="gdn_conv",
    )(proj, conv_w)


def _each_dot_sel(sel, xs):
    sel3 = jnp.concatenate([sel, sel, sel], axis=1).astype(BF16)
    stacked = [jnp.concatenate(_split3(x), axis=0) for x in xs]
    return [_dot(sel3, s) for s in stacked]


def _each_dot_x3(as_, bs):
    lhs, rhs = [], []
    for a, b in zip(as_, bs):
        ah = a.astype(BF16).astype(F32)
        lhs.append(jnp.concatenate([ah, ah, a - ah], axis=1).astype(BF16))
        bh, bl = _split2(b)
        rhs.append(jnp.concatenate([bh, bl, bh], axis=0))
    return [_dot(l, r) for l, r in zip(lhs, rhs)]


def _each_neumann_inverse(bms, L):
    eye = (lax.broadcasted_iota(jnp.int32, (L, L), 0) ==
           lax.broadcasted_iota(jnp.int32, (L, L), 1)).astype(F32)
    ps = [-bm for bm in bms]
    ts = [eye + p for p in ps]
    for _ in range(int(math.log2(L)) - 1):
        ps = _each_dot_x3(ps, ps)
        ts = [t + d for t, d in zip(ts, _each_dot_x3(ts, ps))]
    return ts


def _gdn_kernel(alog_ref, dtb_ref, q_ref, k_ref, v_ref, z_ref, ab_ref, onorm_ref,
                o_ref, s_out_ref,
                g_scr, b_scr, w_scr, u_scr, eg_scr, a_scr, c_scr, qk_scr, qkm_scr, s_scr,
                *, T, H, hb, n_meta, chunk, prep_unroll):
    n_chunks = (T - n_meta) // chunk
    heads = [(hh, pl.program_id(1) * hb + hh, slice(hh * HEAD, (hh + 1) * HEAD)) for hh in range(hb)]

    ab_parts = _split3(ab_ref[...])
    sel_row = lax.broadcasted_iota(jnp.int32, (2 * H, 2 * HEAD), 0)
    sel_col = lax.broadcasted_iota(jnp.int32, (2 * H, 2 * HEAD), 1)
    for _, h, lanes in heads:
        sel = (sel_row == jnp.where(sel_col < HEAD, h, h + H)).astype(BF16)
        ab_col = (_dot(ab_parts[0], sel) + _dot(ab_parts[1], sel)) + _dot(ab_parts[2], sel)
        decay_rate = jnp.exp(jnp.full((1, HEAD), alog_ref[h], F32))
        g_scr[:, lanes] = -decay_rate * _softplus(ab_col[:, :HEAD] + dtb_ref[h])
        b_scr[:, lanes] = _sigmoid(ab_col[:, HEAD:])

    def prep(offs, L, qk_ref):
        ri = lax.broadcasted_iota(jnp.int32, (L, L), 0)
        ci = lax.broadcasted_iota(jnp.int32, (L, L), 1)
        tri = ri >= ci
        strict = ri > ci
        eye = ri == ci
        jobs = [(pl.ds(off, L), pl.ds(qk_off, L), hh, lanes) for off, qk_off, _ in offs for hh, _, lanes in heads]
        slots = [slot for _, _, slot in offs for _ in heads]
        ks = [k_ref[rows, lanes] for rows, _, _, lanes in jobs]
        betas = [b_scr[rows, lanes] for rows, _, _, lanes in jobs]
        gcs = _each_dot_sel(tri.astype(F32), [g_scr[rows, lanes] for rows, _, _, lanes in jobs])
        gc_is = [gc[:, :L] for gc in gcs]
        gc_js = _each_dot_sel(jnp.ones((L, L), F32), [jnp.where(eye, g, 0.0) for g in gc_is])
        decays = [jnp.where(tri, jnp.exp(jnp.where(tri, gi - gj, 0.0)), 0.0) for gi, gj in zip(gc_is, gc_js)]
        kbs = [k.astype(BF16) for k in ks]
        kks = [_dot_nt(kb, kb) for kb in kbs]
        bms = [jnp.where(strict, kk * d, 0.0) * beta[:, :L] for kk, d, beta in zip(kks, decays, betas)]
        ts = _each_neumann_inverse(bms, L)
        egs = [jnp.exp(gc) for gc in gcs]
        wus = _each_dot_x3(ts, [jnp.concatenate([beta * eg * k, beta * v_ref[rows, lanes]], axis=1)
                                for beta, eg, k, (rows, _, _, lanes) in zip(betas, egs, ks, jobs)])
        ws = [wu[:, :HEAD] for wu in wus]
        us = [wu[:, HEAD:] for wu in wus]
        qks = [_dot_nt(q_ref[rows, lanes].astype(BF16), kb) * d
               for (rows, _, _, lanes), kb, d in zip(jobs, kbs, decays)]
        kts = [(k * jnp.exp(gc[L - 1:L, :] - gc)).astype(BF16) for k, gc in zip(ks, gcs)]
        acs = [_dot_tn(kt, wu.astype(BF16)) for kt, wu in zip(kts, wus)]
        for (rows, qk_rows, hh, lanes), slot, w, u, eg, ac, qk in zip(jobs, slots, ws, us, egs, acs, qks):
            w_scr[rows, lanes] = w
            u_scr[rows, lanes] = u
            eg_scr[rows, lanes] = eg
            a_scr[hh, slot] = ac[:, :HEAD].astype(BF16)
            c_scr[hh, slot] = ac[:, HEAD:]
            qk_ref[hh, qk_rows, :] = qk

    def advance(off, L, slot):
        rows = pl.ds(off, L)
        ss = [s_scr[hh] for hh, _, _ in heads]
        lhs = [jnp.concatenate([a_scr[hh, slot], w_scr[rows, lanes].astype(BF16),
                                q_ref[rows, lanes].astype(BF16)], axis=0) for hh, _, lanes in heads]
        prods = [_dot(l, s.astype(BF16)) for l, s in zip(lhs, ss)]
        for (hh, _, lanes), s, p in zip(heads, ss, prods):
            eg_last = eg_scr[pl.ds(off + L - 8, 8), lanes][7:8, :]
            s_scr[hh] = (eg_last * s - p[:HEAD]) + c_scr[hh, slot]
        return tuple(x for (_, _, lanes), p in zip(heads, prods)
                     for x in (u_scr[rows, lanes] - p[HEAD:HEAD + L], p[HEAD + L:]))

    def emit(off, L, qk_ref, qk_off, carried):
        rows = pl.ds(off, L)
        for i, (hh, _, lanes) in enumerate(heads):
            u, qs = carried[2 * i], carried[2 * i + 1]
            o = eg_scr[rows, lanes] * qs + _dot(qk_ref[hh, pl.ds(qk_off, L), :].astype(BF16), u.astype(BF16))
            on = o * lax.rsqrt(jnp.mean(o * o, axis=-1, keepdims=True) + EPS) * onorm_ref[...]
            o_ref[rows, lanes] = (on * _silu(z_ref[rows, lanes])).astype(o_ref.dtype)

    def chunk_off(c):
        off = n_meta + c * chunk
        return off if isinstance(c, int) else pl.multiple_of(off, BF16_SUBLANES)

    def chunk_qk_off(c):
        return c * chunk if isinstance(c, int) else pl.multiple_of(c * chunk, chunk)

    prep([(0, 0, 0)], n_meta, qkm_scr)

    def prep_body(i, carry):
        cs = [i * prep_unroll + j for j in range(prep_unroll)]
        prep([(chunk_off(c), chunk_qk_off(c), c + 1) for c in cs], chunk, qk_scr)
        return carry

    lax.fori_loop(0, n_chunks // prep_unroll, prep_body, 0)

    s_scr[...] = jnp.zeros_like(s_scr)
    emit(0, n_meta, qkm_scr, 0, advance(0, n_meta, 0))

    def step(c, carried):
        new = advance(chunk_off(c), chunk, c + 1)
        emit(chunk_off(c - 1), chunk, qk_scr, chunk_qk_off(c - 1), carried)
        return new

    last = lax.fori_loop(1, n_chunks, step, advance(chunk_off(0), chunk, 1))
    emit(chunk_off(n_chunks - 1), chunk, qk_scr, chunk_qk_off(n_chunks - 1), last)
    s_out_ref[...] = s_scr[...]


def gdn_prompt(qkv_c, proj, ab, a_log, dt_bias, onorm, D):
    B, T, _ = qkv_c.shape
    H = D // HEAD
    assert (T - N_META_TOK) % GDN_CHUNK == 0
    n_chunks = (T - N_META_TOK) // GDN_CHUNK
    hb = 2 if H % 2 == 0 else 1
    prep_unroll = _divisor(n_chunks, 8, 1)
    W = hb * HEAD
    col = lambda part: pl.BlockSpec((None, T, W), lambda b, g: (b, 0, part * (H // hb) + g))
    smem = pl.BlockSpec(memory_space=pltpu.SMEM)
    vmem = ((8 + 5) * T * W * 4 + 2 * T * LANES * 4 + 2 * T * W * 2 + hb * n_chunks * GDN_CHUNK * LANES * 4
            + hb * (n_chunks + 1) * HEAD * HEAD * 6)
    return pl.pallas_call(
        functools.partial(_gdn_kernel, T=T, H=H, hb=hb, n_meta=N_META_TOK, chunk=GDN_CHUNK,
                          prep_unroll=prep_unroll),
        out_shape=[jax.ShapeDtypeStruct((B, T, D), BF16),
                   jax.ShapeDtypeStruct((B, H, HEAD, HEAD), F32)],
        grid=(B, H // hb),
        in_specs=[smem, smem, col(0), col(1), col(2), col(3),
                  pl.BlockSpec((None, T, 2 * H), lambda b, g: (b, 0, 0)),
                  pl.BlockSpec((1, HEAD), lambda b, g: (0, 0))],
        out_specs=[pl.BlockSpec((None, T, W), lambda b, g: (b, 0, g)),
                   pl.BlockSpec((None, hb, HEAD, HEAD), lambda b, g: (b, g, 0, 0))],
        scratch_shapes=[pltpu.VMEM((T, W), F32)] * 5 + [
            pltpu.VMEM((hb, n_chunks + 1, HEAD, HEAD), BF16),
            pltpu.VMEM((hb, n_chunks + 1, HEAD, HEAD), F32),
            pltpu.VMEM((hb, n_chunks * GDN_CHUNK, GDN_CHUNK), F32),
            pltpu.VMEM((hb, N_META_TOK, N_META_TOK), F32),
            pltpu.VMEM((hb, HEAD, HEAD), F32)],
        compiler_params=_params(("parallel", "parallel"), vmem),
        name="gdn_chunk",
    )(a_log.astype(F32), dt_bias.astype(F32), qkv_c, qkv_c, qkv_c, proj, ab, onorm.reshape(1, HEAD).astype(F32))


def _gdn_step_kernel(u_ref, cs_ref, w_ref, ab_ref, alog_ref, dtb_ref, onorm_ref, s_ref,
                     o_ref, s_out_ref, q_scr, k_scr, v_scr, g_scr, b_scr, o_scr, *, H):
    u_all = u_ref[...]
    u = u_all[:3 * H]
    xc = (cs_ref[0] * w_ref[0] + cs_ref[1] * w_ref[1] + cs_ref[2] * w_ref[2]) + u * w_ref[3]
    xc = _silu(xc)
    q = xc[:H]
    k = xc[H:2 * H]
    q_scr[...] = q * lax.rsqrt(jnp.sum(q * q, axis=-1, keepdims=True) + EPS) * (HEAD ** -0.5)
    k_scr[...] = k * lax.rsqrt(jnp.sum(k * k, axis=-1, keepdims=True) + EPS)
    v_scr[...] = xc[2 * H:]
    g_scr[...] = jnp.broadcast_to(-jnp.exp(alog_ref[...]) * _softplus(ab_ref[0] + dtb_ref[...]), (H, HEAD))
    b_scr[...] = jnp.broadcast_to(_sigmoid(ab_ref[1]), (H, HEAD))
    eye = (lax.broadcasted_iota(jnp.int32, (HEAD, HEAD), 0) ==
           lax.broadcasted_iota(jnp.int32, (HEAD, HEAD), 1)).astype(F32)

    group = 8 if H % 8 == 0 else 1

    def heads_step(gi, carry):
        hs = [gi * group + j for j in range(group)]
        rows = [pl.ds(h, 1) for h in hs]
        k_rows = [k_scr[r, :] for r in rows]
        ss = [s_ref[h] * jnp.exp(g_scr[r, :]) for h, r in zip(hs, rows)]
        kss = [_dot(k_row, s, HI) for k_row, s in zip(k_rows, ss)]
        upds = [b_scr[r, :] * (v_scr[r, :] - ks) for r, ks in zip(rows, kss)]
        k_cols = [_dot_nt(eye, jnp.broadcast_to(k_row, (HEAD, HEAD)), HI) for k_row in k_rows]
        ss = [s + k_col * upd for s, k_col, upd in zip(ss, k_cols, upds)]
        outs = [_dot(q_scr[r, :], s, HI) for r, s in zip(rows, ss)]
        for h, r, s, o in zip(hs, rows, ss, outs):
            s_out_ref[h] = s
            o_scr[r, :] = o
        return carry

    lax.fori_loop(0, H // group, heads_step, 0)
    o = o_scr[...]
    on = o * lax.rsqrt(jnp.mean(o * o, axis=-1, keepdims=True) + EPS) * onorm_ref[...]
    o_ref[...] = (on * _silu(u_all[3 * H:])).astype(o_ref.dtype)


def gdn_sample(proj, ab, state_rec, state_conv, conv_w, a_log, dt_bias, onorm, D):
    Bd = proj.shape[0]
    H = D // HEAD
    col = lambda x: x.astype(F32).reshape(H, 1)
    vmem = 2 * (4 * H * HEAD * 4 + 3 * 3 * H * HEAD * 4 + 2 * H * HEAD * HEAD * 4) + 4 * 3 * H * HEAD * 4
    return pl.pallas_call(
        functools.partial(_gdn_step_kernel, H=H),
        out_shape=[jax.ShapeDtypeStruct((Bd, H, HEAD), BF16),
                   jax.ShapeDtypeStruct((Bd, H, HEAD, HEAD), F32)],
        grid=(Bd,),
        in_specs=[pl.BlockSpec((None, 4 * H, HEAD), lambda b: (b, 0, 0)),
                  pl.BlockSpec((None, CONV_TAPS - 1, 3 * H, HEAD), lambda b: (b, 0, 0, 0)),
                  pl.BlockSpec((CONV_TAPS, 3 * H, HEAD), lambda b: (0, 0, 0)),
                  pl.BlockSpec((None, 2, H, 1), lambda b: (b, 0, 0, 0)),
                  pl.BlockSpec((H, 1), lambda b: (0, 0)),
                  pl.BlockSpec((H, 1), lambda b: (0, 0)),
                  pl.BlockSpec((1, HEAD), lambda b: (0, 0)),
                  pl.BlockSpec((None, H, HEAD, HEAD), lambda b: (b, 0, 0, 0))],
        out_specs=[pl.BlockSpec((None, H, HEAD), lambda b: (b, 0, 0)),
                   pl.BlockSpec((None, H, HEAD, HEAD), lambda b: (b, 0, 0, 0))],
        scratch_shapes=[pltpu.VMEM((H, HEAD), F32)] * 6,
        compiler_params=_params(("parallel",), vmem),
        name="gdn_step",
    )(proj.reshape(Bd, 4 * H, HEAD), state_conv.reshape(Bd, CONV_TAPS - 1, 3 * H, HEAD),
      conv_w.reshape(CONV_TAPS, 3 * H, HEAD), ab.reshape(Bd, 2, H, 1), col(a_log), col(dt_bias),
      onorm.reshape(1, HEAD).astype(F32), state_rec)


def _logf_kernel(fl_ref, bf_ref, lf_ref, cum_ref, *, T):
    carry = jnp.zeros((1, fl_ref.shape[-1]), F32)
    for r0 in range(0, T, LANES):
        n = min(LANES, T - r0)
        lf = _log_sigmoid(fl_ref[r0:r0 + n, :] + bf_ref[...])
        lf_ref[r0:r0 + n, :] = lf
        tri = (lax.broadcasted_iota(jnp.int32, (n, n), 0) >=
               lax.broadcasted_iota(jnp.int32, (n, n), 1)).astype(F32)
        cum = _dot(tri, lf, HI) + carry
        cum_ref[r0:r0 + n, :] = cum
        carry = cum[n - 1:n, :]


def fox_logf(fl, b_f):
    B, T, H = fl.shape
    spec = pl.BlockSpec((None, T, H), lambda b: (b, 0, 0))
    return pl.pallas_call(
        functools.partial(_logf_kernel, T=T),
        out_shape=[jax.ShapeDtypeStruct((B, T, H), F32)] * 2,
        grid=(B,),
        in_specs=[spec, pl.BlockSpec((1, H), lambda b: (0, 0))],
        out_specs=[spec, spec],
        compiler_params=_params(("parallel",), 6 * T * LANES * 4),
        name="fox_logf",
    )(fl, b_f.reshape(1, H).astype(F32))


def _fox_kernel(q_ref, k_ref, v_ref, cum_ref, o_ref, kb_scr, vb_scr, cq_scr, ckm_scr, ckt_scr,
                *, T, H, bq):
    h = pl.program_id(1)
    t_main = (T // bq) * bq
    tail = T - t_main
    scale = HEAD ** -0.5
    kb_scr[...] = k_ref[...].astype(BF16)
    vb_scr[...] = v_ref[...].astype(BF16)
    cum = cum_ref[...]
    cq_scr[...] = _dot_sel_r(cum, (lax.broadcasted_iota(jnp.int32, (H, HEAD), 0) == h).astype(BF16))
    pick = (lax.broadcasted_iota(jnp.int32, (8, H), 1) == h).astype(BF16)
    ckm_scr[...] = _dot_sel(pick, cum[:t_main], _dot_nt)
    if tail:
        ckt_scr[...] = _dot_sel(pick, cum[t_main:], _dot_nt)

    def attend(r0, nq, ck_diag):
        rows = slice(r0, r0 + nq)
        qb = q_ref[rows, :].astype(BF16)
        cq = cq_scr[rows, 0:1]
        causal = (lax.broadcasted_iota(jnp.int32, (nq, nq), 0) >=
                  lax.broadcasted_iota(jnp.int32, (nq, nq), 1))
        s_d = jnp.where(causal, _dot_nt(qb, kb_scr[rows, :]) * scale + cq - ck_diag, -jnp.inf)
        m = jnp.max(s_d, axis=-1, keepdims=True)
        if r0:
            s_o = _dot_nt(qb, kb_scr[:r0, :]) * scale + cq - ckm_scr[0:1, :r0]
            m = jnp.maximum(m, jnp.max(s_o, axis=-1, keepdims=True))
        e_d = jnp.exp(s_d - m)
        l = jnp.sum(e_d, axis=-1, keepdims=True)
        o = _dot(e_d.astype(BF16), vb_scr[rows, :])
        if r0:
            e_o = jnp.exp(s_o - m)
            l = l + jnp.sum(e_o, axis=-1, keepdims=True)
            o = o + _dot(e_o.astype(BF16), vb_scr[:r0, :])
        o_ref[rows, :] = (o / l).astype(o_ref.dtype)

    for r0 in range(0, t_main, bq):
        attend(r0, bq, ckm_scr[0:1, r0:r0 + bq])
    if tail:
        attend(t_main, tail, ckt_scr[0:1, :])


def fox_prompt(q, k, v, cum):
    B, T, D = q.shape
    H = D // HEAD
    bq = 2 * LANES
    t_main = (T // bq) * bq
    tail = T - t_main
    assert tail % BF16_SUBLANES == 0
    head = pl.BlockSpec((None, T, HEAD), lambda b, h: (b, 0, h))
    vmem = 6 * T * HEAD * 4 + 2 * T * LANES * 4 + 2 * T * HEAD * 2 + 2 * T * HEAD * 2 + T * HEAD * 4 \
        + 8 * T * 4 + 6 * bq * t_main * 4
    return pl.pallas_call(
        functools.partial(_fox_kernel, T=T, H=H, bq=bq),
        out_shape=jax.ShapeDtypeStruct((B, T, D), BF16),
        grid=(B, H),
        in_specs=[head, head, head, pl.BlockSpec((None, T, H), lambda b, h: (b, 0, 0))],
        out_specs=head,
        scratch_shapes=[pltpu.VMEM((T, HEAD), BF16), pltpu.VMEM((T, HEAD), BF16),
                        pltpu.VMEM((T, HEAD), F32), pltpu.VMEM((8, t_main), F32),
                        pltpu.VMEM((8, max(tail, 8)), F32)],
        compiler_params=_params(("parallel", "parallel"), vmem),
        name="fox_prompt",
    )(q, k, v, cum)


def _page_suffix_kernel(lf_ref, within_ref, total_ref, *, pages, H):
    x = lf_ref[...]
    W = x.shape[1]
    lane = lax.broadcasted_iota(jnp.int32, x.shape, 1)
    later = x
    earlier = x
    s = H
    while s < W:
        later = later + jnp.where(lane < W - s, pltpu.roll(later, W - s, 1), 0.0)
        earlier = earlier + jnp.where(lane >= s, pltpu.roll(earlier, s, 1), 0.0)
        s *= 2
    within = later - x
    total = within + earlier
    for p in range(pages):
        within_ref[p] = within[p:p + 1, :]
        total_ref[p] = total[p:p + 1, :]


def fox_page_suffix(cache_logf, layer):
    _, n_pool, _, H = cache_logf.shape
    W = PAGE * H
    pages = _divisor(n_pool, 16, 8)
    spec = pl.BlockSpec((pages, 1, W), lambda i: (i, 0, 0))
    return pl.pallas_call(
        functools.partial(_page_suffix_kernel, pages=pages, H=H),
        out_shape=[jax.ShapeDtypeStruct((n_pool, 1, W), F32)] * 2,
        grid=(n_pool // pages,),
        in_specs=[pl.BlockSpec((pages, W), lambda i: (i, 0))],
        out_specs=[spec, spec],
        compiler_params=_params(("parallel",), 12 * pages * W * 4),
        name="fox_page_suffix",
    )(cache_logf[layer].reshape(n_pool, W))


def _logsig_kernel(x_ref, b_ref, o_ref):
    o_ref[...] = _log_sigmoid(x_ref[...] + b_ref[...])


def fox_logf_step(fl, b_f):
    return pl.pallas_call(
        _logsig_kernel, out_shape=jax.ShapeDtypeStruct(fl.shape, F32), name="fox_logf_step",
    )(fl, b_f.reshape(1, -1).astype(F32))


def _fox_decode_kernel(pt_ref, *refs, n_pages, G, H):
    k_refs = refs[0:G]
    v_refs = refs[G:2 * G]
    w_refs = refs[2 * G:3 * G]
    t_refs = refs[3 * G:4 * G]
    q_ref, kn_ref, vn_ref, lfn_ref, lfc_ref, o_ref, m_scr, l_scr, acc_scr, carry_scr = refs[4 * G:]
    g = pl.program_id(1)
    scale = HEAD ** -0.5
    W = PAGE * H
    q = q_ref[...]
    qb = q.astype(BF16)
    lane = lax.broadcasted_iota(jnp.int32, (H, W), 1)
    lane_head = jnp.bitwise_and(lane, H - 1) if H & (H - 1) == 0 else lane % H
    own = lane_head == lax.broadcasted_iota(jnp.int32, (H, W), 0)

    @pl.when(g == 0)
    def _():
        m_scr[...] = jnp.full_like(m_scr, -jnp.inf)
        l_scr[...] = jnp.zeros_like(l_scr)
        acc_scr[...] = jnp.zeros_like(acc_scr)
        carry_scr[...] = jnp.zeros_like(carry_scr)

    def online_update(s, pv):
        m_old = m_scr[...]
        m_new = jnp.maximum(m_old, jnp.max(s, axis=-1, keepdims=True))
        alpha = jnp.exp(m_old - m_new)
        p = jnp.exp(s - m_new)
        l_scr[...] = l_scr[...] * alpha + jnp.sum(p, axis=-1, keepdims=True)
        acc_scr[...] = acc_scr[...] * alpha + pv(p)
        m_scr[...] = m_new

    for i in range(G):
        kb = k_refs[i][...].reshape(W, HEAD).astype(BF16)
        vb = v_refs[i][...].reshape(W, HEAD).astype(BF16)
        bias = lfn_ref[...] + (w_refs[i][...] + carry_scr[...])
        carry_scr[...] = carry_scr[...] + t_refs[i][...]
        s = jnp.where(own, _dot_nt(qb, kb) * scale + bias, -jnp.inf)
        online_update(s, lambda p: _dot(p.astype(BF16), vb))

    @pl.when(g == pl.num_programs(1) - 1)
    def _():
        cum_new = lfc_ref[...]
        s_new = jnp.sum(q * kn_ref[...], axis=-1, keepdims=True) * scale + (cum_new - cum_new)
        online_update(s_new, lambda p: p * vn_ref[...])
        o_ref[...] = (acc_scr[...] / l_scr[...]).astype(o_ref.dtype)


def fox_decode(q, k_new, v_new, fl_new, b_f, cache_k, cache_v, cache_logf, layer, page_table, D):
    Bd = q.shape[0]
    H = D // HEAD
    n_pages = page_table.shape[1]
    G = 4 if n_pages % 4 == 0 else 1
    W = PAGE * H
    lf_new = fox_logf_step(fl_new, b_f)
    within, total = fox_page_suffix(cache_logf, layer)

    def page_spec(block):
        def make(i):
            def index(b, g, pt):
                return block[0](pt[b, n_pages - 1 - (g * G + i)])
            return pl.BlockSpec(block[1], index)
        return [make(i) for i in range(G)]

    kv = ((lambda p: (layer, p, 0, 0, 0)), (None, None, PAGE, H, HEAD))
    flat = ((lambda p: (p, 0, 0)), (None, 1, W))
    per_seq = lambda shape: pl.BlockSpec((None,) + shape, lambda b, g, pt: (b, 0, 0))
    vmem = 2 * 2 * G * PAGE * D * 4 + 2 * G * PAGE * D * 2 + 4 * G * 8 * W * 4 + 10 * H * W * 4
    grid_spec = pltpu.PrefetchScalarGridSpec(
        num_scalar_prefetch=1,
        grid=(Bd, n_pages // G),
        in_specs=(page_spec(kv) + page_spec(kv) + page_spec(flat) + page_spec(flat)
                  + [per_seq((H, HEAD))] * 3 + [per_seq((1, W)), per_seq((H, 1))]),
        out_specs=per_seq((H, HEAD)),
        scratch_shapes=[pltpu.VMEM((H, 1), F32), pltpu.VMEM((H, 1), F32), pltpu.VMEM((H, HEAD), F32),
                        pltpu.VMEM((1, W), F32)],
    )
    heads = lambda x: x.reshape(Bd, H, HEAD)
    o = pl.pallas_call(
        functools.partial(_fox_decode_kernel, n_pages=n_pages, G=G, H=H),
        out_shape=jax.ShapeDtypeStruct((Bd, H, HEAD), BF16),
        grid_spec=grid_spec,
        compiler_params=_params(("parallel", "arbitrary"), vmem),
        name="fox_decode",
    )(page_table, *([cache_k] * G), *([cache_v] * G), *([within] * G), *([total] * G),
      heads(q), heads(k_new), heads(v_new), jnp.tile(lf_new, (1, PAGE)).reshape(Bd, 1, W),
      lf_new.reshape(Bd, H, 1))
    return o.reshape(Bd, D), lf_new


def _ffn(hp, hs, norm_w, w_gu, w_down, layer):
    act_p, act_s = matmul_swiglu(rmsnorm(hp, norm_w, BF16), w_gu, layer, sample=(rmsnorm(hs, norm_w, BF16), None))
    return matmul(act_p, w_down, layer, residual=hp, sample=(act_s, hs))


def kernel(x_prompt, x_sample, cache_k, cache_v, cache_logf, page_table, state_rec, state_conv,
           meta_tokens, norm_mix, norm_ffn, norm_final, w_in_a, conv_w_a, a_log_a, dt_bias_a,
           onorm_a, w_out_a, w_in_b, b_f, w_out_b, w_gu, w_down):
    B, S, D = x_prompt.shape
    Bd, Q, _ = x_sample.shape
    assert Q == 1 and Bd <= SAMPLE_ROWS, "the sample group is a single-token step of a few sequences"
    H = D // HEAD
    T = S + N_META_TOK
    M = B * T
    meta = jnp.broadcast_to(meta_tokens.astype(x_prompt.dtype)[None], (B, N_META_TOK, D))
    hp = jnp.concatenate([meta, x_prompt], axis=1).reshape(M, D)
    pad = lambda a: jnp.pad(a, ((0, SAMPLE_ROWS - a.shape[0]), (0, 0)))
    hs = pad(x_sample.reshape(Bd, D))
    bf = lambda w: w.astype(BF16)
    w_in_a, w_out_a, w_in_b, w_out_b, w_down = map(bf, (w_in_a, w_out_a, w_in_b, w_out_b, w_down))

    w_ab = w_in_a[:, :, 4 * D:]
    ap = rmsnorm(hp, norm_mix[0], BF16)
    a_s = rmsnorm(hs, norm_mix[0], BF16)
    proj_p, proj_s = matmul(ap, w_in_a, cols=(0, 4 * D), sample=(a_s, None))
    ab_p, ab_s = matmul(ap, w_ab, sample=(a_s, None))
    proj_p = proj_p.reshape(B, T, 4 * D)
    proj_s = proj_s[:Bd]

    qkv_c = gdn_conv_prompt(proj_p, conv_w_a[0], D)
    on_p, rec_p = gdn_prompt(qkv_c, proj_p, ab_p.reshape(B, T, 2 * H), a_log_a[0], dt_bias_a[0], onorm_a[0], D)
    conv_p = proj_p[:, T - (CONV_TAPS - 1):, :3 * D]
    on_s, rec_s = gdn_sample(proj_s, ab_s[:Bd], state_rec[0], state_conv[0], conv_w_a[0], a_log_a[0],
                             dt_bias_a[0], onorm_a[0], D)
    conv_s = jnp.concatenate([state_conv[0][:, 1:], proj_s[:, None, :3 * D].astype(state_conv.dtype)], axis=1)

    hp, hs = matmul(on_p.reshape(M, D), w_out_a, residual=hp, sample=(pad(on_s.reshape(Bd, D)), hs))
    hp, hs = _ffn(hp, hs, norm_ffn[0], w_gu, w_down, 0)

    w_f = w_in_b[:, :, 3 * D:]
    ap = rmsnorm(hp, norm_mix[1], BF16)
    a_s = rmsnorm(hs, norm_mix[1], BF16)
    (q_p, q_s), (k_p, k_s), (v_p, v_s) = (matmul(ap, w_in_b, cols=(part * D, D), sample=(a_s, None))
                                          for part in range(3))
    q_p, k_p, v_p = (x.reshape(B, T, D) for x in (q_p, k_p, v_p))
    q_s, k_s, v_s = (x[:Bd] for x in (q_s, k_s, v_s))
    fl_p, fl_s = matmul(ap, w_f, sample=(a_s, None))

    lf_p, cum_p = fox_logf(fl_p.reshape(B, T, H), b_f[0])
    o_p = fox_prompt(q_p, k_p, v_p, cum_p)
    o_s, lf_s = fox_decode(q_s, k_s, v_s, fl_s[:Bd], b_f[0], cache_k, cache_v, cache_logf, 0, page_table, D)

    hp, hs = matmul(o_p.reshape(M, D), w_out_b, residual=hp, sample=(pad(o_s), hs))
    hp, hs = _ffn(hp, hs, norm_ffn[1], w_gu, w_down, 1)

    y_prompt = rmsnorm_tail(hp.reshape(B, T, D), norm_final, N_META_TOK)
    y_sample = rmsnorm(hs, norm_final, F32)[:Bd].reshape(Bd, 1, D)

    k_prompt = k_p.reshape(1, B, T, H, HEAD)
    v_prompt = v_p.reshape(1, B, T, H, HEAD)
    logf_prompt = lf_p.astype(cache_logf.dtype)[None]
    k_sample = k_s.reshape(1, Bd, 1, H, HEAD)
    v_sample = v_s.reshape(1, Bd, 1, H, HEAD)
    logf_sample = lf_s.astype(cache_logf.dtype).reshape(1, Bd, 1, H)
    return (y_prompt, y_sample, k_prompt, v_prompt, logf_prompt, k_sample, v_sample, logf_sample,
            rec_p[None], conv_p[None], rec_s.astype(state_rec.dtype)[None], conv_s[None])
```
